```python
import jax, jax.numpy as jnp
from jax import lax
import numpy as np

D_MODEL = 4096
BATCH = 32
SEQ = 256
DEPTH = 4
DEC_BATCH = 2
DEC_SEQ = 1024
PAST_LEN = 512

GRID_W = 64
N_MIXERS = 3
EPS = 1e-6

MLSTM_HEADS = 8
MLSTM_DQK = D_MODEL // (2 * MLSTM_HEADS)
MLSTM_DV = D_MODEL // MLSTM_HEADS
MLSTM_CHUNK = 64

SSD_INNER = 2 * D_MODEL
SSD_HEAD_DIM = 64
SSD_HEADS = SSD_INNER // SSD_HEAD_DIM
SSD_GROUPS = 8
SSD_STATE = 128
SSD_CONV = 4
SSD_CHUNK = 64

MLA_HEADS = 64
MLA_Q_RANK = 1536
MLA_KV_RANK = 512
MLA_NOPE = 128
MLA_ROPE = 64
MLA_V = 128
ROPE_THETA = 10000.0
ROPE_FREQS = MLA_ROPE // 4
Q_BLOCK = 128

MOE_GROUPS = 8
MOE_PER_GROUP = 8
N_EXPERTS = MOE_GROUPS * MOE_PER_GROUP
MOE_TOP_K = 2
MOE_FF = 512
MOE_BLOCK = 128

kernel_name = 'hybrid_mlstm_ssd_mla_hmoe_dit'


def _kind_counts():
    return [len(range(k, DEPTH, N_MIXERS)) for k in range(N_MIXERS)]


def rmsnorm(x, w):
    xf = x.astype(jnp.float32)
    y = xf * lax.rsqrt(jnp.mean(xf * xf, axis=-1, keepdims=True) + EPS)
    return (y * w.astype(jnp.float32)).astype(x.dtype)


def modulate(h, shift, scale):
    return h * (1 + scale) + shift


def axial_rope_tables(n_tok):
    rows = n_tok // GRID_W
    row = jnp.repeat(jnp.arange(rows, dtype=jnp.float32), GRID_W)
    col = jnp.tile(jnp.arange(GRID_W, dtype=jnp.float32), rows)
    inv_freq = ROPE_THETA ** (-jnp.arange(ROPE_FREQS, dtype=jnp.float32) / ROPE_FREQS)
    ang = jnp.stack([row, col], axis=-1)[:, :, None] * inv_freq
    return jnp.cos(ang), jnp.sin(ang)


def apply_axial_rope(x, cos, sin):
    xr = x.astype(jnp.float32).reshape(*x.shape[:-1], 2, 2, ROPE_FREQS)
    x1, x2 = xr[..., 0, :], xr[..., 1, :]
    out = jnp.stack([x1 * cos - x2 * sin, x2 * cos + x1 * sin], axis=-2)
    return out.reshape(x.shape).astype(x.dtype)


def mlstm_scan(q, k, v, ig, lf, C0, n0, m0):
    B_, S_ = q.shape[:2]
    L = MLSTM_CHUNK
    nc = S_ // L
    tril = jnp.tril(jnp.ones((L, L), dtype=bool))[None, :, :, None]

    def chunks(a):
        return jnp.swapaxes(a.reshape(B_, nc, L, *a.shape[2:]), 0, 1)

    def step(carry, inp):
        C, n, m = carry
        qc, kc, vc, ic, fc = inp
        b = jnp.cumsum(fc, axis=1)
        log_d = jnp.where(tril, b[:, :, None, :] - b[:, None, :, :] + ic[:, None, :, :], -jnp.inf)
        log_inter = b + m[:, None, :]
        m_t = jnp.maximum(log_inter, jnp.max(log_d, axis=2))
        w_intra = jnp.exp(log_d - m_t[:, :, None, :])
        w_inter = jnp.exp(log_inter - m_t)
        qk = jnp.einsum('bthd,bshd->btsh', qc, kc) * w_intra
        num = jnp.einsum('btsh,bshv->bthv', qk, vc) + w_inter[..., None] * jnp.einsum('bthd,bhdv->bthv', qc, C)
        den = jnp.sum(qk, axis=2) + w_inter * jnp.einsum('bthd,bhd->bth', qc, n)
        h = num / jnp.maximum(jnp.abs(den), jnp.exp(-m_t))[..., None]
        b_end = b[:, -1, :]
        log_w = b_end[:, None, :] - b + ic
        m_new = jnp.maximum(b_end + m, jnp.max(log_w, axis=1))
        kw = kc * jnp.exp(log_w - m_new[:, None, :])[..., None]
        decay = jnp.exp(b_end + m - m_new)
        C_new = decay[..., None, None] * C + jnp.einsum('bshd,bshv->bhdv', kw, vc)
        n_new = decay[..., None] * n + jnp.sum(kw, axis=1)
        return (C_new, n_new, m_new), h

    (C, n, m), hs = lax.scan(step, (C0, n0, m0), tuple(chunks(a) for a in (q, k, v, ig, lf)))
    return jnp.swapaxes(hs, 0, 1).reshape(B_, S_, MLSTM_HEADS, MLSTM_DV), (C, n, m)


def mlstm_mixer(h, C0, n0, m0, w_in, b_gates, norm_w, w_out):
    f32 = jnp.float32
    B_, S_, _ = h.shape
    qk_w, v_w = MLSTM_HEADS * MLSTM_DQK, MLSTM_HEADS * MLSTM_DV
    q, k, v, o, g = jnp.split(h @ w_in, [qk_w, 2 * qk_w, 2 * qk_w + v_w, 2 * qk_w + 2 * v_w], axis=-1)
    q = q.astype(f32).reshape(B_, S_, MLSTM_HEADS, MLSTM_DQK) * (MLSTM_DQK ** -0.5)
    k = k.astype(f32).reshape(B_, S_, MLSTM_HEADS, MLSTM_DQK)
    v = v.astype(f32).reshape(B_, S_, MLSTM_HEADS, MLSTM_DV)
    g = (g.astype(f32) + b_gates.astype(f32)).reshape(B_, S_, 2, 2, MLSTM_HEADS)
    ig, lf = g[..., 0, :], jax.nn.log_sigmoid(g[..., 1, :])
    C0, n0, m0 = C0.astype(f32), n0.astype(f32), m0.astype(f32)
    h_f, (Cf, nf, mf) = mlstm_scan(q, k, v, ig[:, :, 0], lf[:, :, 0], C0[:, 0], n0[:, 0], m0[:, 0])
    rev = lambda a: jnp.flip(a, axis=1)
    h_b, (Cb, nb, mb) = mlstm_scan(rev(q), rev(k), rev(v), rev(ig[:, :, 1]), rev(lf[:, :, 1]),
                                   C0[:, 1], n0[:, 1], m0[:, 1])
    hsum = h_f + rev(h_b)
    hn = hsum * lax.rsqrt(jnp.mean(hsum * hsum, axis=-1, keepdims=True) + EPS)
    hn = hn * norm_w.astype(f32).reshape(MLSTM_HEADS, MLSTM_DV)
    out = (hn.reshape(B_, S_, v_w).astype(h.dtype) * jax.nn.sigmoid(o)) @ w_out
    return out, (jnp.stack([Cf, Cb], axis=1), jnp.stack([nf, nb], axis=1), jnp.stack([mf, mb], axis=1))


def ssd_scan(x, dt, A, Bm, Cm, h0):
    B_, S_, H, P = x.shape
    G, N = Bm.shape[2], Bm.shape[3]
    E = H // G
    L = SSD_CHUNK
    nc = S_ // L
    tril = jnp.tril(jnp.ones((L, L), dtype=bool))[None, :, :, None, None]
    A = A.reshape(G, E)

    def chunks(a):
        return jnp.swapaxes(a.reshape(B_, nc, L, *a.shape[2:]), 0, 1)

    def step(h, inp):
        xc, dtc, Bc, Cc = inp
        cs = jnp.cumsum(dtc * A, axis=1)
        decay = jnp.exp(jnp.where(tril, cs[:, :, None] - cs[:, None, :], -jnp.inf))
        cb = jnp.einsum('btgn,bsgn->btsg', Cc, Bc)
        xdt = xc * dtc[..., None]
        y = jnp.einsum('btsge,bsgep->btgep', cb[..., None] * decay, xdt)
        y = y + jnp.exp(cs)[..., None] * jnp.einsum('btgn,bgepn->btgep', Cc, h)
        cs_end = cs[:, -1]
        xw = xdt * jnp.exp(cs_end[:, None] - cs)[..., None]
        h_new = jnp.exp(cs_end)[..., None, None] * h + jnp.einsum('bsgep,bsgn->bgepn', xw, Bc)
        return h_new, y

    inp = (chunks(x.reshape(B_, S_, G, E, P)), chunks(dt.reshape(B_, S_, G, E)), chunks(Bm), chunks(Cm))
    h_end, ys = lax.scan(step, h0.reshape(B_, G, E, P, N), inp)
    return jnp.swapaxes(ys, 0, 1).reshape(B_, S_, H, P), h_end.reshape(B_, H, P, N)


def centred_depthwise_conv(x, w, b):
    ch = x.shape[-1]
    left = SSD_CONV // 2
    y = lax.conv_general_dilated(x, w[:, None, :].astype(x.dtype), window_strides=(1,),
                                 padding=[(left, SSD_CONV - 1 - left)],
                                 dimension_numbers=('NWC', 'WIO', 'NWC'), feature_group_count=ch)
    return y + b.astype(x.dtype)


def ssd_mixer(h, h0, w_in, conv_w, conv_b, dt_bias, A_log, D_skip, norm_w, w_out):
    f32 = jnp.float32
    B_, S_, _ = h.shape
    gn = SSD_GROUPS * SSD_STATE
    z, xbc, dt_raw = jnp.split(h @ w_in, [SSD_INNER, 2 * SSD_INNER + 2 * gn], axis=-1)
    xbc = jax.nn.silu(centred_depthwise_conv(xbc, conv_w, conv_b))
    xs, Bm, Cm = jnp.split(xbc.astype(f32), [SSD_INNER, SSD_INNER + gn], axis=-1)
    xs = xs.reshape(B_, S_, SSD_HEADS, SSD_HEAD_DIM)
    Bm = Bm.reshape(B_, S_, SSD_GROUPS, SSD_STATE)
    Cm = Cm.reshape(B_, S_, SSD_GROUPS, SSD_STATE)
    dt = jax.nn.softplus(dt_raw.astype(f32).reshape(B_, S_, 2, SSD_HEADS) + dt_bias.astype(f32))
    A = -jnp.exp(A_log.astype(f32))
    h0 = h0.astype(f32)
    y_f, s_f = ssd_scan(xs, dt[:, :, 0], A[0], Bm, Cm, h0[:, 0])
    rev = lambda a: jnp.flip(a, axis=1)
    y_b, s_b = ssd_scan(rev(xs), rev(dt[:, :, 1]), A[1], rev(Bm), rev(Cm), h0[:, 1])
    y = y_f + rev(y_b) + D_skip.astype(f32)[:, None] * xs
    y = y.reshape(B_, S_, SSD_INNER) * jax.nn.silu(z.astype(f32))
    yg = y.reshape(B_, S_, SSD_GROUPS, SSD_INNER // SSD_GROUPS)
    yg = yg * lax.rsqrt(jnp.mean(yg * yg, axis=-1, keepdims=True) + EPS)
    y = yg.reshape(B_, S_, SSD_INNER) * norm_w.astype(f32)
    return y.astype(h.dtype) @ w_out, jnp.stack([s_f, s_b], axis=1)


def mla_project(h, w_in, q_norm, kv_norm, w_qb):
    B_, S_, _ = h.shape
    q_lat, ckv, kpe = jnp.split(h @ w_in, [MLA_Q_RANK, MLA_Q_RANK + MLA_KV_RANK], axis=-1)
    q = (rmsnorm(q_lat, q_norm) @ w_qb).reshape(B_, S_, MLA_HEADS, MLA_NOPE + MLA_ROPE)
    return q[..., :MLA_NOPE], q[..., MLA_NOPE:], rmsnorm(ckv, kv_norm), kpe


def mla_expand(ckv, w_kvb):
    B_, L_, _ = ckv.shape
    kv = (ckv @ w_kvb).reshape(B_, L_, MLA_HEADS, MLA_NOPE + MLA_V)
    return kv[..., :MLA_NOPE], kv[..., MLA_NOPE:]


def mla_attend(q_nope, q_pe, k_nope, k_pe, v):
    B_, Sq = q_nope.shape[:2]
    nb = Sq // Q_BLOCK
    scale = (MLA_NOPE + MLA_ROPE) ** -0.5

    def blocks(a):
        return jnp.swapaxes(a.reshape(B_, nb, Q_BLOCK, *a.shape[2:]), 0, 1)

    def one_block(qs):
        qn, qp = qs
        s = (jnp.einsum('bqhd,bkhd->bhqk', qn, k_nope).astype(jnp.float32)
             + jnp.einsum('bqhr,bkr->bhqk', qp, k_pe).astype(jnp.float32))
        p = jax.nn.softmax(s * scale, axis=-1).astype(v.dtype)
        return jnp.einsum('bhqk,bkhd->bqhd', p, v)

    o = lax.map(one_block, (blocks(q_nope), blocks(q_pe)))
    return jnp.swapaxes(o, 0, 1).reshape(B_, Sq, MLA_HEADS * MLA_V)


def mla_context(h, w_in, q_norm, kv_norm, w_qb, w_kvb, w_out):
    qn, qp, ckv, kpe = mla_project(h, w_in, q_norm, kv_norm, w_qb)
    kn, v = mla_expand(ckv, w_kvb)
    return mla_attend(qn, qp, kn, kpe, v) @ w_out, ckv, kpe


def mla_latent(h, ctx_ckv, ctx_kpe, cos, sin, w_in, q_norm, kv_norm, w_qb, w_kvb, w_out):
    qn, qp, ckv, kpe = mla_project(h, w_in, q_norm, kv_norm, w_qb)
    qp = apply_axial_rope(qp, cos[:, None], sin[:, None])
    kpe = apply_axial_rope(kpe, cos, sin)
    ckv_all = jnp.concatenate([ctx_ckv.astype(h.dtype), ckv], axis=1)
    kpe_all = jnp.concatenate([ctx_kpe.astype(h.dtype), kpe], axis=1)
    kn, v = mla_expand(ckv_all, w_kvb)
    return mla_attend(qn, qp, kn, kpe_all, v) @ w_out


def hier_moe(h, w_router_group, w_router_expert, w_gate, w_up, w_down):
    shape = h.shape
    xt = h.reshape(-1, shape[-1])
    T = xt.shape[0]
    g_logits = (xt @ w_router_group).astype(jnp.float32)
    g_sel = jnp.argmax(g_logits, axis=-1)
    g_prob = jnp.take_along_axis(jax.nn.softmax(g_logits, axis=-1), g_sel[:, None], axis=1)
    e_logits = (xt @ w_router_expert).astype(jnp.float32).reshape(T, MOE_GROUPS, MOE_PER_GROUP)
    e_in_group = jnp.take_along_axis(e_logits, g_sel[:, None, None], axis=1)[:, 0]
    top_val, top_idx = lax.top_k(e_in_group, MOE_TOP_K)
    gates = (g_prob * jax.nn.softmax(top_val, axis=-1)).reshape(-1)
    expert = (g_sel[:, None] * MOE_PER_GROUP + top_idx).reshape(-1).astype(jnp.int32)
    token = jnp.repeat(jnp.arange(T, dtype=jnp.int32), MOE_TOP_K)
    n_assign = T * MOE_TOP_K
    order = jnp.argsort(expert)
    e_sorted = expert[order]
    counts = jnp.bincount(expert, length=N_EXPERTS)
    starts = jnp.cumsum(counts) - counts
    padded = (counts + MOE_BLOCK - 1) // MOE_BLOCK * MOE_BLOCK
    p_ends = jnp.cumsum(padded)
    p_starts = p_ends - padded
    dest = p_starts[e_sorted] + jnp.arange(n_assign, dtype=jnp.int32) - starts[e_sorted]
    n_blocks = -(-n_assign // MOE_BLOCK) + N_EXPERTS
    n_rows = n_blocks * MOE_BLOCK
    row_token = jnp.zeros((n_rows,), jnp.int32).at[dest].set(token[order])
    row_gate = jnp.zeros((n_rows,), h.dtype).at[dest].set(gates[order].astype(h.dtype))
    block_expert = jnp.minimum(
        jnp.searchsorted(p_ends, jnp.arange(n_blocks, dtype=jnp.int32) * MOE_BLOCK, side='right'),
        N_EXPERTS - 1)
    xb = xt[row_token].reshape(n_blocks, MOE_BLOCK, shape[-1])

    def expert_block(args):
        xblk, e = args
        return (jax.nn.silu(xblk @ w_gate[e]) * (xblk @ w_up[e])) @ w_down[e]

    yb = lax.map(expert_block, (xb, block_expert))
    y = jnp.zeros_like(xt).at[row_token].add(yb.reshape(n_rows, shape[-1]) * row_gate[:, None])
    return y.reshape(shape)


def setup_inputs(seed: int = 0) -> dict:
    key = jax.random.key(seed)
    ks = iter(jax.random.split(key, 64))
    f32 = jnp.float32
    D = D_MODEL
    n_a, n_b, n_c = _kind_counts()

    def nrm(shape, scale=1.0):
        return scale * jax.random.normal(next(ks), shape, f32)

    def gain(shape):
        return 1.0 + 0.02 * jax.random.normal(next(ks), shape, f32)

    mlstm_in = 2 * MLSTM_HEADS * MLSTM_DQK + 2 * MLSTM_HEADS * MLSTM_DV + 4 * MLSTM_HEADS
    ssd_in = 2 * SSD_INNER + 2 * SSD_GROUPS * SSD_STATE + 2 * SSD_HEADS
    ssd_conv_ch = SSD_INNER + 2 * SSD_GROUPS * SSD_STATE
    mla_in = MLA_Q_RANK + MLA_KV_RANK + MLA_ROPE

    i_bias = nrm((n_a, 2, 1, MLSTM_HEADS), 0.1)
    f_bias = 3.0 + nrm((n_a, 2, 1, MLSTM_HEADS), 0.1)
    mlstm_b_gates = jnp.concatenate([i_bias, f_bias], axis=2).reshape(n_a, 4 * MLSTM_HEADS)
    dt0 = jnp.exp(jax.random.uniform(next(ks), (n_b, 2, SSD_HEADS), f32, np.log(1e-3), np.log(1e-1)))
    ssd_dt_bias = dt0 + jnp.log(-jnp.expm1(-dt0))
    ssd_A_log = jnp.log(jax.random.uniform(next(ks), (n_b, 2, SSD_HEADS), f32, 1.0, 16.0))

    return {
        'x_prompt': nrm((BATCH, SEQ, D)),
        'x_sample': nrm((DEC_BATCH, DEC_SEQ, D)),
        'state_mlstm_C': nrm((DEC_BATCH, n_a, 2, MLSTM_HEADS, MLSTM_DQK, MLSTM_DV)),
        'state_mlstm_n': nrm((DEC_BATCH, n_a, 2, MLSTM_HEADS, MLSTM_DQK)),
        'state_mlstm_m': nrm((DEC_BATCH, n_a, 2, MLSTM_HEADS), 0.5),
        'state_ssm': nrm((DEC_BATCH, n_b, 2, SSD_HEADS, SSD_HEAD_DIM, SSD_STATE), 0.1),
        'cache_mla_ckv': nrm((DEC_BATCH, n_c, PAST_LEN, MLA_KV_RANK)),
        'cache_mla_kpe': nrm((DEC_BATCH, n_c, PAST_LEN, MLA_ROPE)),
        'c': nrm((DEC_BATCH, D)),
        'c_ctx': nrm((D,)),
        'ada_w': nrm((DEPTH, D, 6 * D), 0.5 * D ** -0.5),
        'ada_b': nrm((DEPTH, 6 * D), 0.02),
        'norm_mix': gain((DEPTH, D)),
        'norm_ffn': gain((DEPTH, D)),
        'mlstm_w_in': nrm((n_a, D, mlstm_in), D ** -0.5),
        'mlstm_b_gates': mlstm_b_gates,
        'mlstm_norm': gain((n_a, MLSTM_HEADS * MLSTM_DV)),
        'mlstm_w_out': nrm((n_a, MLSTM_HEADS * MLSTM_DV, D), (MLSTM_HEADS * MLSTM_DV) ** -0.5),
        'ssd_w_in': nrm((n_b, D, ssd_in), D ** -0.5),
        'ssd_conv_w': nrm((n_b, SSD_CONV, ssd_conv_ch), SSD_CONV ** -0.5),
        'ssd_conv_b': nrm((n_b, ssd_conv_ch), 0.02),
        'ssd_dt_bias': ssd_dt_bias,
        'ssd_A_log': ssd_A_log,
        'ssd_D': gain((n_b, SSD_HEADS)),
        'ssd_norm': gain((n_b, SSD_INNER)),
        'ssd_w_out': nrm((n_b, SSD_INNER, D), SSD_INNER ** -0.5),
        'mla_w_in': nrm((n_c, D, mla_in), D ** -0.5),
        'mla_q_norm': gain((n_c, MLA_Q_RANK)),
        'mla_kv_norm': gain((n_c, MLA_KV_RANK)),
        'mla_w_qb': nrm((n_c, MLA_Q_RANK, MLA_HEADS * (MLA_NOPE + MLA_ROPE)), MLA_Q_RANK ** -0.5),
        'mla_w_kvb': nrm((n_c, MLA_KV_RANK, MLA_HEADS * (MLA_NOPE + MLA_V)), MLA_KV_RANK ** -0.5),
        'mla_w_out': nrm((n_c, MLA_HEADS * MLA_V, D), (MLA_HEADS * MLA_V) ** -0.5),
        'router_group_w': nrm((DEPTH, D, MOE_GROUPS), D ** -0.5),
        'router_expert_w': nrm((DEPTH, D, N_EXPERTS), D ** -0.5),
        'moe_w_gate': nrm((DEPTH, N_EXPERTS, D, MOE_FF), D ** -0.5),
        'moe_w_up': nrm((DEPTH, N_EXPERTS, D, MOE_FF), D ** -0.5),
        'moe_w_down': nrm((DEPTH, N_EXPERTS, MOE_FF, D), MOE_FF ** -0.5),
        'final_norm': gain((D,)),
    }


def reference(x_prompt, x_sample, state_mlstm_C, state_mlstm_n, state_mlstm_m, state_ssm,
              cache_mla_ckv, cache_mla_kpe, c, c_ctx, ada_w, ada_b, norm_mix, norm_ffn,
              mlstm_w_in, mlstm_b_gates, mlstm_norm, mlstm_w_out,
              ssd_w_in, ssd_conv_w, ssd_conv_b, ssd_dt_bias, ssd_A_log, ssd_D, ssd_norm, ssd_w_out,
              mla_w_in, mla_q_norm, mla_kv_norm, mla_w_qb, mla_w_kvb, mla_w_out,
              router_group_w, router_expert_w, moe_w_gate, moe_w_up, moe_w_down, final_norm):
    f32 = jnp.float32
    bp = x_prompt.shape[0]
    cos, sin = axial_rope_tables(x_sample.shape[1])
    silu_ctx = jax.nn.silu(c_ctx)[None, :]
    silu_c = jax.nn.silu(c)
    xp, xs = x_prompt, x_sample
    new_C, new_n, new_m, new_ssm, new_ckv, new_kpe = [], [], [], [], [], []
    for l in range(DEPTH):
        kind, j = l % N_MIXERS, l // N_MIXERS
        mp = jnp.split((silu_ctx @ ada_w[l] + ada_b[l])[:, None, :].astype(xp.dtype), 6, axis=-1)
        ms = jnp.split((silu_c @ ada_w[l] + ada_b[l])[:, None, :].astype(xs.dtype), 6, axis=-1)
        hp = modulate(rmsnorm(xp, norm_mix[l]), mp[0], mp[1])
        hs = modulate(rmsnorm(xs, norm_mix[l]), ms[0], ms[1])
        if kind == 0:
            w = (mlstm_w_in[j], mlstm_b_gates[j], mlstm_norm[j], mlstm_w_out[j])
            z_C = jnp.zeros((bp, 2, MLSTM_HEADS, MLSTM_DQK, MLSTM_DV), f32)
            z_n = jnp.zeros((bp, 2, MLSTM_HEADS, MLSTM_DQK), f32)
            z_m = jnp.zeros((bp, 2, MLSTM_HEADS), f32)
            op, (sC, sn, sm) = mlstm_mixer(hp, z_C, z_n, z_m, *w)
            os_, _ = mlstm_mixer(hs, state_mlstm_C[:, j], state_mlstm_n[:, j], state_mlstm_m[:, j], *w)
            new_C.append(sC)
            new_n.append(sn)
            new_m.append(sm)
        elif kind == 1:
            w = (ssd_w_in[j], ssd_conv_w[j], ssd_conv_b[j], ssd_dt_bias[j], ssd_A_log[j], ssd_D[j],
                 ssd_norm[j], ssd_w_out[j])
            z_h = jnp.zeros((bp, 2, SSD_HEADS, SSD_HEAD_DIM, SSD_STATE), f32)
            op, s_h = ssd_mixer(hp, z_h, *w)
            os_, _ = ssd_mixer(hs, state_ssm[:, j], *w)
            new_ssm.append(s_h)
        else:
            w = (mla_w_in[j], mla_q_norm[j], mla_kv_norm[j], mla_w_qb[j], mla_w_kvb[j], mla_w_out[j])
            op, ckv, kpe = mla_context(hp, *w)
            os_ = mla_latent(hs, cache_mla_ckv[:, j], cache_mla_kpe[:, j], cos, sin, *w)
            new_ckv.append(ckv)
            new_kpe.append(kpe)
        xp = xp + mp[2] * op
        xs = xs + ms[2] * os_
        moe_w = (router_group_w[l], router_expert_w[l], moe_w_gate[l], moe_w_up[l], moe_w_down[l])
        xp = xp + mp[5] * hier_moe(modulate(rmsnorm(xp, norm_ffn[l]), mp[3], mp[4]), *moe_w)
        xs = xs + ms[5] * hier_moe(modulate(rmsnorm(xs, norm_ffn[l]), ms[3], ms[4]), *moe_w)
    y_prompt = rmsnorm(xp, final_norm)
    y_sample = rmsnorm(xs, final_norm)
    new_mlstm_C = jnp.stack(new_C, axis=1)
    new_mlstm_n = jnp.stack(new_n, axis=1)
    new_mlstm_m = jnp.stack(new_m, axis=1)
    new_ssm_state = jnp.stack(new_ssm, axis=1)
    new_mla_ckv = jnp.stack(new_ckv, axis=1)
    new_mla_kpe = jnp.stack(new_kpe, axis=1)
    return (y_prompt, y_sample, new_mlstm_C, new_mlstm_n, new_mlstm_m, new_ssm_state, new_mla_ckv, new_mla_kpe)
```

```python
import functools

import jax
import jax.numpy as jnp
from jax import lax
from jax.experimental import pallas as pl
from jax.experimental.pallas import tpu as pltpu

D_MODEL = 4096
BATCH = 32
SEQ = 256
DEPTH = 4
DEC_BATCH = 2
DEC_SEQ = 1024
PAST_LEN = 512
GRID_W = 64
N_MIXERS = 3
EPS = 1e-6

MLSTM_HEADS = 8
MLSTM_DQK = 256
MLSTM_DV = 512
MLSTM_CHUNK = 64

SSD_INNER = 8192
SSD_HEAD_DIM = 64
SSD_HEADS = 128
SSD_GROUPS = 8
SSD_STATE = 128
SSD_CONV = 4
SSD_CHUNK = 64

MLA_HEADS = 64
MLA_Q_RANK = 1536
MLA_KV_RANK = 512
MLA_NOPE = 128
MLA_ROPE = 64
MLA_V = 128
ROPE_THETA = 10000.0
ROPE_FREQS = 16
Q_BLOCK = 128

MOE_GROUPS = 8
MOE_PER_GROUP = 8
N_EXPERTS = 64
MOE_TOP_K = 2
MOE_FF = 512

N_PROMPT = BATCH * SEQ
N_SAMPLE = DEC_BATCH * DEC_SEQ
N_TOK = N_PROMPT + N_SAMPLE
N_MOD = 6

V7X_VMEM_LIMIT = 56 * 1024 * 1024
MOE_BLOCK = 128
MOE_FF_SPLIT = 2
ROUTER_LANES = 128


def _params(*sem):
    return pltpu.CompilerParams(dimension_semantics=sem, vmem_limit_bytes=V7X_VMEM_LIMIT)


def _row_group(tile, rows_per_tile):
    first_sample = N_PROMPT // rows_per_tile
    per_batch = DEC_SEQ // rows_per_tile
    return jnp.where(tile < first_sample, 0, 1 + (tile - first_sample) // per_batch)


def _mm_kernel(*refs, has_res):
    if has_res:
        x_ref, w_ref, res_ref, gate_ref, o_ref, wbf_ref = refs
    else:
        x_ref, w_ref, o_ref, wbf_ref = refs

    @pl.when(pl.program_id(1) == 0)
    def _():
        wbf_ref[...] = w_ref[...].astype(jnp.bfloat16)

    acc = jnp.dot(x_ref[...], wbf_ref[...], preferred_element_type=jnp.float32)
    if has_res:
        acc = res_ref[...] + gate_ref[...] * acc
    o_ref[...] = acc.astype(o_ref.dtype)


def _matmul(x, w, *, layer=None, col0=0, n=None, tm=1024, tn=512, out_dtype=jnp.bfloat16,
            res=None, mods=None, gate_chunk=None):
    m, k = x.shape
    if w.ndim == 2:
        w = w[None]
        layer = 0
    n = w.shape[2] - col0 if n is None else n
    tm = min(tm, m)
    assert m % tm == 0 and col0 % tn == 0 and w.shape[1] == k
    cb0 = col0 // tn
    grid = (pl.cdiv(n, tn), m // tm)
    in_specs = [
        pl.BlockSpec((tm, k), lambda j, i: (i, 0)),
        pl.BlockSpec((None, k, tn), lambda j, i: (layer, 0, cb0 + j)),
    ]
    args = [x, w]
    has_res = res is not None
    if has_res:
        in_specs += [
            pl.BlockSpec((tm, tn), lambda j, i: (i, j)),
            pl.BlockSpec((None, None, 1, tn), lambda j, i: (_row_group(i, tm), gate_chunk, 0, j)),
        ]
        args += [res, mods]
    n_out = grid[0] * tn
    return pl.pallas_call(
        functools.partial(_mm_kernel, has_res=has_res),
        grid=grid,
        in_specs=in_specs,
        out_specs=pl.BlockSpec((tm, tn), lambda j, i: (i, j)),
        out_shape=jax.ShapeDtypeStruct((m, n_out), out_dtype),
        scratch_shapes=[pltpu.VMEM((k, tn), jnp.bfloat16)],
        compiler_params=_params("parallel", "arbitrary"),
    )(*args)


def _norm_kernel(*refs, has_mod, has_router):
    it = iter(refs)
    x_ref, w_ref = next(it), next(it)
    shift_ref = scale_ref = wr_ref = None
    if has_mod:
        shift_ref, scale_ref = next(it), next(it)
    if has_router:
        wr_ref = next(it)
    o_ref = next(it)
    x = x_ref[...].astype(jnp.float32)
    y = x * lax.rsqrt(jnp.mean(x * x, axis=-1, keepdims=True) + EPS) * w_ref[...]
    if has_mod:
        y = y * (1.0 + scale_ref[...]) + shift_ref[...]
    o_ref[...] = y.astype(o_ref.dtype)
    if has_router:
        logits_ref = next(it)
        logits_ref[...] = jnp.dot(y, wr_ref[...], preferred_element_type=jnp.float32,
                                  precision=lax.Precision.HIGHEST)


def _norm(x, w, *, mods=None, shift_chunk=None, scale_chunk=None, router_w=None, tm=256,
          out_dtype=jnp.bfloat16):
    m, d = x.shape
    has_mod = mods is not None
    has_router = router_w is not None
    in_specs = [pl.BlockSpec((tm, d), lambda i: (i, 0)), pl.BlockSpec((1, d), lambda i: (0, 0))]
    args = [x, w.reshape(1, d)]
    if has_mod:
        in_specs += [
            pl.BlockSpec((None, None, 1, d), lambda i: (_row_group(i, tm), shift_chunk, 0, 0)),
            pl.BlockSpec((None, None, 1, d), lambda i: (_row_group(i, tm), scale_chunk, 0, 0)),
        ]
        args += [mods, mods]
    out_specs = pl.BlockSpec((tm, d), lambda i: (i, 0))
    out_shape = jax.ShapeDtypeStruct((m, d), out_dtype)
    if has_router:
        in_specs.append(pl.BlockSpec((d, ROUTER_LANES), lambda i: (0, 0)))
        args.append(router_w)
        out_specs = [out_specs, pl.BlockSpec((tm, ROUTER_LANES), lambda i: (i, 0))]
        out_shape = [out_shape, jax.ShapeDtypeStruct((m, ROUTER_LANES), jnp.float32)]
    return pl.pallas_call(
        functools.partial(_norm_kernel, has_mod=has_mod, has_router=has_router),
        grid=(m // tm,),
        in_specs=in_specs,
        out_specs=out_specs,
        out_shape=out_shape,
        compiler_params=_params("parallel"),
    )(*args)


def _resid_kernel(x_ref, y_ref, gate_ref, o_ref):
    o_ref[...] = x_ref[...] + gate_ref[...] * y_ref[...].astype(jnp.float32)


def _gated_residual(x, y, mods, gate_chunk, tm=256):
    m, d = x.shape
    return pl.pallas_call(
        _resid_kernel,
        grid=(m // tm,),
        in_specs=[
            pl.BlockSpec((tm, d), lambda i: (i, 0)),
            pl.BlockSpec((tm, d), lambda i: (i, 0)),
            pl.BlockSpec((None, None, 1, d), lambda i: (_row_group(i, tm), gate_chunk, 0, 0)),
        ],
        out_specs=pl.BlockSpec((tm, d), lambda i: (i, 0)),
        out_shape=jax.ShapeDtypeStruct((m, d), jnp.float32),
        compiler_params=_params("parallel"),
    )(x, y, mods)


def _moe_kernel(be_ref, first_ref, valid_ref, x_ref, wg_ref, wu_ref, wd_ref, o_ref, wg_bf, wu_bf, wd_bf):
    i = pl.program_id(1)

    @pl.when(first_ref[i] == 1)
    def _():
        wg_bf[...] = wg_ref[...].astype(jnp.bfloat16)
        wu_bf[...] = wu_ref[...].astype(jnp.bfloat16)
        wd_bf[...] = wd_ref[...].astype(jnp.bfloat16)

    @pl.when(valid_ref[i] == 1)
    def _():
        x = x_ref[...]
        g = jnp.dot(x, wg_bf[...], preferred_element_type=jnp.float32)
        u = jnp.dot(x, wu_bf[...], preferred_element_type=jnp.float32)
        a = (g * jax.nn.sigmoid(g) * u).astype(jnp.bfloat16)
        o_ref[...] = jnp.dot(a, wd_bf[...], preferred_element_type=jnp.float32)

    @pl.when(valid_ref[i] == 0)
    def _():
        o_ref[...] = jnp.zeros_like(o_ref)


def _moe_ffn(xg, block_expert, block_first, block_valid, w_gate, w_up, w_down, layer):
    n_rows, d = xg.shape
    n_blocks = n_rows // MOE_BLOCK
    ffs = MOE_FF // MOE_FF_SPLIT
    grid_spec = pltpu.PrefetchScalarGridSpec(
        num_scalar_prefetch=3,
        grid=(MOE_FF_SPLIT, n_blocks),
        in_specs=[
            pl.BlockSpec((MOE_BLOCK, d), lambda j, i, be, bf, bv: (i, 0)),
            pl.BlockSpec((None, None, d, ffs), lambda j, i, be, bf, bv: (layer, be[i], 0, j)),
            pl.BlockSpec((None, None, d, ffs), lambda j, i, be, bf, bv: (layer, be[i], 0, j)),
            pl.BlockSpec((None, None, ffs, d), lambda j, i, be, bf, bv: (layer, be[i], j, 0)),
        ],
        out_specs=pl.BlockSpec((None, MOE_BLOCK, d), lambda j, i, be, bf, bv: (j, i, 0)),
        scratch_shapes=[
            pltpu.VMEM((d, ffs), jnp.bfloat16),
            pltpu.VMEM((d, ffs), jnp.bfloat16),
            pltpu.VMEM((ffs, d), jnp.bfloat16),
        ],
    )
    return pl.pallas_call(
        _moe_kernel,
        grid_spec=grid_spec,
        out_shape=jax.ShapeDtypeStruct((MOE_FF_SPLIT, n_rows, d), jnp.float32),
        compiler_params=_params("arbitrary", "arbitrary"),
    )(block_expert, block_first, block_valid, xg, w_gate, w_up, w_down)


def _route(logits):
    t = logits.shape[0]
    g_logits = logits[:, :MOE_GROUPS]
    g_sel = jnp.argmax(g_logits, axis=-1)
    g_prob = jnp.take_along_axis(jax.nn.softmax(g_logits, axis=-1), g_sel[:, None], axis=1)
    e_logits = logits[:, MOE_GROUPS:MOE_GROUPS + N_EXPERTS].reshape(t, MOE_GROUPS, MOE_PER_GROUP)
    e_in_group = jnp.take_along_axis(e_logits, g_sel[:, None, None], axis=1)[:, 0]
    top_val, top_idx = lax.top_k(e_in_group, MOE_TOP_K)
    gates = g_prob * jax.nn.softmax(top_val, axis=-1)
    expert = (g_sel[:, None] * MOE_PER_GROUP + top_idx).astype(jnp.int32)
    return gates, expert


def _moe_layout(expert):
    n_assign = expert.size
    e_flat = expert.reshape(-1)
    onehot = (e_flat[:, None] == jnp.arange(N_EXPERTS, dtype=jnp.int32)[None, :]).astype(jnp.int32)
    csum = jnp.cumsum(onehot, axis=0)
    rank = jnp.take_along_axis(csum, e_flat[:, None], axis=1)[:, 0] - 1
    counts = csum[-1]
    padded = (counts + MOE_BLOCK - 1) // MOE_BLOCK * MOE_BLOCK
    p_ends = jnp.cumsum(padded)
    p_starts = p_ends - padded
    pos = (p_starts[e_flat] + rank).astype(jnp.int32)
    n_blocks = -(-n_assign // MOE_BLOCK) + N_EXPERTS
    n_rows = n_blocks * MOE_BLOCK
    token = jnp.repeat(jnp.arange(n_assign // MOE_TOP_K, dtype=jnp.int32), MOE_TOP_K)
    row_token = jnp.zeros((n_rows,), jnp.int32).at[pos].set(token)
    block_start = jnp.arange(n_blocks, dtype=jnp.int32) * MOE_BLOCK
    block_expert = jnp.minimum(jnp.searchsorted(p_ends, block_start, side='right'),
                               N_EXPERTS - 1).astype(jnp.int32)
    block_valid = (block_start < p_ends[-1]).astype(jnp.int32)
    prev = jnp.concatenate([jnp.full((1,), -1, jnp.int32), block_expert[:-1]])
    block_first = (block_expert != prev).astype(jnp.int32)
    return pos.reshape(expert.shape), row_token, block_expert, block_first, block_valid


def _hier_moe(h, logits, w_gate, w_up, w_down, layer):
    gates, expert = _route(logits)
    pos, row_token, block_expert, block_first, block_valid = _moe_layout(expert)
    xg = jnp.take(h, row_token, axis=0)
    parts = _moe_ffn(xg, block_expert, block_first, block_valid, w_gate, w_up, w_down, layer)
    yb = parts[0] + parts[1]
    return (jnp.take(yb, pos[:, 0], axis=0) * gates[:, 0:1]
            + jnp.take(yb, pos[:, 1], axis=0) * gates[:, 1:2])


def _axial_rope_tables(n_tok):
    rows = n_tok // GRID_W
    row = jnp.repeat(jnp.arange(rows, dtype=jnp.float32), GRID_W)
    col = jnp.tile(jnp.arange(GRID_W, dtype=jnp.float32), rows)
    inv_freq = ROPE_THETA ** (-jnp.arange(ROPE_FREQS, dtype=jnp.float32) / ROPE_FREQS)
    ang = jnp.stack([row, col], axis=-1)[:, :, None] * inv_freq
    return jnp.cos(ang), jnp.sin(ang)


def _apply_axial_rope(x, cos, sin):
    xr = x.astype(jnp.float32).reshape(*x.shape[:-1], 2, 2, ROPE_FREQS)
    x1, x2 = xr[..., 0, :], xr[..., 1, :]
    out = jnp.stack([x1 * cos - x2 * sin, x2 * cos + x1 * sin], axis=-2)
    return out.reshape(x.shape).astype(x.dtype)


def _mlstm_scan(q, k, v, ig, lf, C0, n0, m0):
    B_, S_ = q.shape[:2]
    L = MLSTM_CHUNK
    nc = S_ // L
    tril = jnp.tril(jnp.ones((L, L), dtype=bool))[None, :, :, None]

    def chunks(a):
        return jnp.swapaxes(a.reshape(B_, nc, L, *a.shape[2:]), 0, 1)

    def step(carry, inp):
        C, n, m = carry
        qc, kc, vc, ic, fc = inp
        b = jnp.cumsum(fc, axis=1)
        log_d = jnp.where(tril, b[:, :, None, :] - b[:, None, :, :] + ic[:, None, :, :], -jnp.inf)
        log_inter = b + m[:, None, :]
        m_t = jnp.maximum(log_inter, jnp.max(log_d, axis=2))
        w_intra = jnp.exp(log_d - m_t[:, :, None, :])
        w_inter = jnp.exp(log_inter - m_t)
        qk = jnp.einsum('bthd,bshd->btsh', qc, kc) * w_intra
        num = jnp.einsum('btsh,bshv->bthv', qk, vc) + w_inter[..., None] * jnp.einsum('bthd,bhdv->bthv', qc, C)
        den = jnp.sum(qk, axis=2) + w_inter * jnp.einsum('bthd,bhd->bth', qc, n)
        h = num / jnp.maximum(jnp.abs(den), jnp.exp(-m_t))[..., None]
        b_end = b[:, -1, :]
        log_w = b_end[:, None, :] - b + ic
        m_new = jnp.maximum(b_end + m, jnp.max(log_w, axis=1))
        kw = kc * jnp.exp(log_w - m_new[:, None, :])[..., None]
        decay = jnp.exp(b_end + m - m_new)
        C_new = decay[..., None, None] * C + jnp.einsum('bshd,bshv->bhdv', kw, vc)
        n_new = decay[..., None] * n + jnp.sum(kw, axis=1)
        return (C_new, n_new, m_new), h

    (C, n, m), hs = lax.scan(step, (C0, n0, m0), tuple(chunks(a) for a in (q, k, v, ig, lf)))
    return jnp.swapaxes(hs, 0, 1).reshape(B_, S_, MLSTM_HEADS, MLSTM_DV), (C, n, m)


def _mlstm_seq(proj, gates, C0, n0, m0, norm_w):
    f32 = jnp.float32
    B_, S_, _ = proj.shape
    qk_w, v_w = MLSTM_HEADS * MLSTM_DQK, MLSTM_HEADS * MLSTM_DV
    q, k, v, o = jnp.split(proj, [qk_w, 2 * qk_w, 2 * qk_w + v_w], axis=-1)
    q = q.astype(f32).reshape(B_, S_, MLSTM_HEADS, MLSTM_DQK) * (MLSTM_DQK ** -0.5)
    k = k.astype(f32).reshape(B_, S_, MLSTM_HEADS, MLSTM_DQK)
    v = v.astype(f32).reshape(B_, S_, MLSTM_HEADS, MLSTM_DV)
    g = gates.reshape(B_, S_, 2, 2, MLSTM_HEADS)
    ig, lf = g[..., 0, :], jax.nn.log_sigmoid(g[..., 1, :])
    h_f, (Cf, nf, mf) = _mlstm_scan(q, k, v, ig[:, :, 0], lf[:, :, 0], C0[:, 0], n0[:, 0], m0[:, 0])
    rev = lambda a: jnp.flip(a, axis=1)
    h_b, (Cb, nb, mb) = _mlstm_scan(rev(q), rev(k), rev(v), rev(ig[:, :, 1]), rev(lf[:, :, 1]),
                                    C0[:, 1], n0[:, 1], m0[:, 1])
    hsum = h_f + rev(h_b)
    hn = hsum * lax.rsqrt(jnp.mean(hsum * hsum, axis=-1, keepdims=True) + EPS)
    hn = hn * norm_w.astype(f32).reshape(MLSTM_HEADS, MLSTM_DV)
    out = hn.reshape(B_, S_, v_w) * jax.nn.sigmoid(o.astype(f32))
    states = (jnp.stack([Cf, Cb], axis=1), jnp.stack([nf, nb], axis=1), jnp.stack([mf, mb], axis=1))
    return out.astype(jnp.bfloat16), states


def _ssd_scan(x, dt, A, Bm, Cm, h0):
    B_, S_, H, P = x.shape
    G, N = Bm.shape[2], Bm.shape[3]
    E = H // G
    L = SSD_CHUNK
    nc = S_ // L
    tril = jnp.tril(jnp.ones((L, L), dtype=bool))[None, :, :, None, None]
    A = A.reshape(G, E)

    def chunks(a):
        return jnp.swapaxes(a.reshape(B_, nc, L, *a.shape[2:]), 0, 1)

    def step(h, inp):
        xc, dtc, Bc, Cc = inp
        cs = jnp.cumsum(dtc * A, axis=1)
        decay = jnp.exp(jnp.where(tril, cs[:, :, None] - cs[:, None, :], -jnp.inf))
        cb = jnp.einsum('btgn,bsgn->btsg', Cc, Bc)
        xdt = xc * dtc[..., None]
        y = jnp.einsum('btsge,bsgep->btgep', cb[..., None] * decay, xdt)
        y = y + jnp.exp(cs)[..., None] * jnp.einsum('btgn,bgepn->btgep', Cc, h)
        cs_end = cs[:, -1]
        xw = xdt * jnp.exp(cs_end[:, None] - cs)[..., None]
        h_new = jnp.exp(cs_end)[..., None, None] * h + jnp.einsum('bsgep,bsgn->bgepn', xw, Bc)
        return h_new, y

    inp = (chunks(x.reshape(B_, S_, G, E, P)), chunks(dt.reshape(B_, S_, G, E)), chunks(Bm), chunks(Cm))
    h_end, ys = lax.scan(step, h0.reshape(B_, G, E, P, N), inp)
    return jnp.swapaxes(ys, 0, 1).reshape(B_, S_, H, P), h_end.reshape(B_, H, P, N)


def _ssd_seq(zx, dt_raw, h0, conv_w, conv_b, dt_bias, A_log, D_skip, norm_w):
    f32 = jnp.float32
    B_, S_, _ = zx.shape
    gn = SSD_GROUPS * SSD_STATE
    z = zx[..., :SSD_INNER].astype(f32)
    xbc = zx[..., SSD_INNER:].astype(f32)
    left = SSD_CONV // 2
    conv = lax.conv_general_dilated(xbc, conv_w[:, None, :], window_strides=(1,),
                                    padding=[(left, SSD_CONV - 1 - left)],
                                    dimension_numbers=('NWC', 'WIO', 'NWC'),
                                    feature_group_count=xbc.shape[-1])
    xbc = jax.nn.silu(conv + conv_b)
    xs, Bm, Cm = jnp.split(xbc, [SSD_INNER, SSD_INNER + gn], axis=-1)
    xs = xs.reshape(B_, S_, SSD_HEADS, SSD_HEAD_DIM)
    Bm = Bm.reshape(B_, S_, SSD_GROUPS, SSD_STATE)
    Cm = Cm.reshape(B_, S_, SSD_GROUPS, SSD_STATE)
    dt = jax.nn.softplus(dt_raw.reshape(B_, S_, 2, SSD_HEADS) + dt_bias.astype(f32))
    A = -jnp.exp(A_log.astype(f32))
    y_f, s_f = _ssd_scan(xs, dt[:, :, 0], A[0], Bm, Cm, h0[:, 0])
    rev = lambda a: jnp.flip(a, axis=1)
    y_b, s_b = _ssd_scan(rev(xs), rev(dt[:, :, 1]), A[1], rev(Bm), rev(Cm), h0[:, 1])
    y = y_f + rev(y_b) + D_skip.astype(f32)[:, None] * xs
    y = y.reshape(B_, S_, SSD_INNER) * jax.nn.silu(z)
    yg = y.reshape(B_, S_, SSD_GROUPS, SSD_INNER // SSD_GROUPS)
    yg = yg * lax.rsqrt(jnp.mean(yg * yg, axis=-1, keepdims=True) + EPS)
    y = yg.reshape(B_, S_, SSD_INNER) * norm_w.astype(f32)
    return y.astype(jnp.bfloat16), jnp.stack([s_f, s_b], axis=1)


def _mla_attend(q_nope, q_pe, k_nope, k_pe, v):
    B_, Sq = q_nope.shape[:2]
    nb = Sq // Q_BLOCK
    scale = (MLA_NOPE + MLA_ROPE) ** -0.5

    def blocks(a):
        return jnp.swapaxes(a.reshape(B_, nb, Q_BLOCK, *a.shape[2:]), 0, 1)

    def one_block(qs):
        qn, qp = qs
        s = (jnp.einsum('bqhd,bkhd->bhqk', qn, k_nope, preferred_element_type=jnp.float32)
             + jnp.einsum('bqhr,bkr->bhqk', qp, k_pe, preferred_element_type=jnp.float32))
        p = jax.nn.softmax(s * scale, axis=-1).astype(v.dtype)
        return jnp.einsum('bhqk,bkhd->bqhd', p, v, preferred_element_type=jnp.float32)

    o = lax.map(one_block, (blocks(q_nope), blocks(q_pe)))
    return jnp.swapaxes(o, 0, 1).reshape(B_, Sq, MLA_HEADS * MLA_V)


def kernel(x_prompt, x_sample, state_mlstm_C, state_mlstm_n, state_mlstm_m, state_ssm, cache_mla_ckv, cache_mla_kpe, c, c_ctx, ada_w, ada_b, norm_mix, norm_ffn, mlstm_w_in, mlstm_b_gates, mlstm_norm, mlstm_w_out, ssd_w_in, ssd_conv_w, ssd_conv_b, ssd_dt_bias, ssd_A_log, ssd_D, ssd_norm, ssd_w_out, mla_w_in, mla_q_norm, mla_kv_norm, mla_w_qb, mla_w_kvb, mla_w_out, router_group_w, router_expert_w, moe_w_gate, moe_w_up, moe_w_down, final_norm):
    f32, bf16 = jnp.float32, jnp.bfloat16
    D = D_MODEL
    X = jnp.concatenate([x_prompt.reshape(N_PROMPT, D), x_sample.reshape(N_SAMPLE, D)], axis=0)

    cond = jnp.concatenate([c_ctx[None, :], c], axis=0)
    cond = jnp.pad(jax.nn.silu(cond), ((0, 16 - cond.shape[0]), (0, 0))).astype(bf16)

    cos, sin = _axial_rope_tables(DEC_SEQ)
    new_C, new_n, new_m, new_ssm, new_ckv, new_kpe = [], [], [], [], [], []
    for l in range(DEPTH):
        kind, j = l % N_MIXERS, l // N_MIXERS
        ada = _matmul(cond, ada_w, layer=l, tm=16, tn=512, out_dtype=f32)
        mods = (ada[:1 + DEC_BATCH] + ada_b[l]).reshape(1 + DEC_BATCH, N_MOD, 1, D)
        h = _norm(X, norm_mix[l], mods=mods, shift_chunk=0, scale_chunk=1)

        if kind == 0:
            proj = _matmul(h, mlstm_w_in, layer=j, n=12288)
            gates = _matmul(h, mlstm_w_in, layer=j, col0=12288, n=128, tn=128, out_dtype=f32)
            gates = gates[:, :4 * MLSTM_HEADS] + mlstm_b_gates[j]
            yp, (sC, sn, sm) = _mlstm_seq(
                proj[:N_PROMPT].reshape(BATCH, SEQ, -1), gates[:N_PROMPT].reshape(BATCH, SEQ, -1),
                jnp.zeros((BATCH, 2, MLSTM_HEADS, MLSTM_DQK, MLSTM_DV), f32),
                jnp.zeros((BATCH, 2, MLSTM_HEADS, MLSTM_DQK), f32),
                jnp.zeros((BATCH, 2, MLSTM_HEADS), f32), mlstm_norm[j])
            ys, _ = _mlstm_seq(
                proj[N_PROMPT:].reshape(DEC_BATCH, DEC_SEQ, -1), gates[N_PROMPT:].reshape(DEC_BATCH, DEC_SEQ, -1),
                state_mlstm_C[:, j], state_mlstm_n[:, j], state_mlstm_m[:, j], mlstm_norm[j])
            new_C.append(sC)
            new_n.append(sn)
            new_m.append(sm)
            y = jnp.concatenate([yp.reshape(N_PROMPT, -1), ys.reshape(N_SAMPLE, -1)], axis=0)
            X = _matmul(y, mlstm_w_out, layer=j, out_dtype=f32, res=X, mods=mods, gate_chunk=2)
        elif kind == 1:
            zx = _matmul(h, ssd_w_in, layer=j, n=18432)
            dt_raw = _matmul(h, ssd_w_in, layer=j, col0=18432, n=256, tn=256, out_dtype=f32)
            w = (ssd_conv_w[j], ssd_conv_b[j], ssd_dt_bias[j], ssd_A_log[j], ssd_D[j], ssd_norm[j])
            yp, s_h = _ssd_seq(zx[:N_PROMPT].reshape(BATCH, SEQ, -1), dt_raw[:N_PROMPT].reshape(BATCH, SEQ, -1),
                               jnp.zeros((BATCH, 2, SSD_HEADS, SSD_HEAD_DIM, SSD_STATE), f32), *w)
            ys, _ = _ssd_seq(zx[N_PROMPT:].reshape(DEC_BATCH, DEC_SEQ, -1),
                             dt_raw[N_PROMPT:].reshape(DEC_BATCH, DEC_SEQ, -1), state_ssm[:, j], *w)
            new_ssm.append(s_h)
            y = jnp.concatenate([yp.reshape(N_PROMPT, -1), ys.reshape(N_SAMPLE, -1)], axis=0)
            X = _matmul(y, ssd_w_out, layer=j, tm=512, tn=256, out_dtype=f32, res=X, mods=mods, gate_chunk=2)
        else:
            lat = _matmul(h, mla_w_in, layer=j, n=2048, out_dtype=f32)
            kpe = _matmul(h, mla_w_in, layer=j, col0=2048, n=128, tn=128, out_dtype=f32)[:, :MLA_ROPE]
            hq = _norm(lat[:, :MLA_Q_RANK], mla_q_norm[j])
            ckv = _norm(lat[:, MLA_Q_RANK:], mla_kv_norm[j], out_dtype=f32)
            q = _matmul(hq, mla_w_qb, layer=j, tn=512)
            ckv_s = jnp.concatenate([cache_mla_ckv[:, j], ckv[N_PROMPT:].reshape(DEC_BATCH, DEC_SEQ, -1)], axis=1)
            ckv_all = jnp.concatenate([ckv[:N_PROMPT], ckv_s.reshape(-1, MLA_KV_RANK)], axis=0).astype(bf16)
            kv = _matmul(ckv_all, mla_w_kvb, layer=j, tn=512)
            n_keys = PAST_LEN + DEC_SEQ
            qh_p = q[:N_PROMPT].reshape(BATCH, SEQ, MLA_HEADS, MLA_NOPE + MLA_ROPE)
            kv_p = kv[:N_PROMPT].reshape(BATCH, SEQ, MLA_HEADS, MLA_NOPE + MLA_V)
            kpe_p = kpe[:N_PROMPT].reshape(BATCH, SEQ, MLA_ROPE)
            o_p = _mla_attend(qh_p[..., :MLA_NOPE], qh_p[..., MLA_NOPE:], kv_p[..., :MLA_NOPE],
                              kpe_p.astype(bf16), kv_p[..., MLA_NOPE:])
            qh_s = q[N_PROMPT:].reshape(DEC_BATCH, DEC_SEQ, MLA_HEADS, MLA_NOPE + MLA_ROPE)
            kv_s = kv[N_PROMPT:].reshape(DEC_BATCH, n_keys, MLA_HEADS, MLA_NOPE + MLA_V)
            qp_s = _apply_axial_rope(qh_s[..., MLA_NOPE:].astype(f32), cos[:, None], sin[:, None]).astype(bf16)
            kpe_s = _apply_axial_rope(kpe[N_PROMPT:].reshape(DEC_BATCH, DEC_SEQ, MLA_ROPE), cos, sin)
            kpe_s = jnp.concatenate([cache_mla_kpe[:, j], kpe_s], axis=1).astype(bf16)
            o_s = _mla_attend(qh_s[..., :MLA_NOPE], qp_s, kv_s[..., :MLA_NOPE], kpe_s, kv_s[..., MLA_NOPE:])
            new_ckv.append(ckv[:N_PROMPT].reshape(BATCH, SEQ, MLA_KV_RANK))
            new_kpe.append(kpe_p)
            y = jnp.concatenate([o_p.reshape(N_PROMPT, -1), o_s.reshape(N_SAMPLE, -1)], axis=0).astype(bf16)
            X = _matmul(y, mla_w_out, layer=j, tm=512, tn=256, out_dtype=f32, res=X, mods=mods, gate_chunk=2)

        router_w = jnp.pad(jnp.concatenate([router_group_w[l], router_expert_w[l]], axis=1),
                           ((0, 0), (0, ROUTER_LANES - MOE_GROUPS - N_EXPERTS)))
        hf, logits = _norm(X, norm_ffn[l], mods=mods, shift_chunk=3, scale_chunk=4, router_w=router_w)
        y = _hier_moe(hf, logits, moe_w_gate, moe_w_up, moe_w_down, l)
        X = _gated_residual(X, y, mods, 5)

    Y = _norm(X, final_norm, out_dtype=f32)
    y_prompt = Y[:N_PROMPT].reshape(BATCH, SEQ, D)
    y_sample = Y[N_PROMPT:].reshape(DEC_BATCH, DEC_SEQ, D)
    return (y_prompt, y_sample, jnp.stack(new_C, axis=1), jnp.stack(new_n, axis=1), jnp.stack(new_m, axis=1),
            jnp.stack(new_ssm, axis=1), jnp.stack(new_ckv, axis=1), jnp.stack(new_kpe, axis=1))
```

```python
import functools

import jax
import jax.numpy as jnp
from jax import lax
from jax.experimental import pallas as pl
from jax.experimental.pallas import tpu as pltpu

D_MODEL = 4096
BATCH = 32
SEQ = 256
DEPTH = 4
DEC_BATCH = 2
DEC_SEQ = 1024
PAST_LEN = 512
GRID_W = 64
N_MIXERS = 3
EPS = 1e-6

MLSTM_HEADS = 8
MLSTM_DQK = 256
MLSTM_DV = 512

SSD_INNER = 8192
SSD_HEAD_DIM = 64
SSD_HEADS = 128
SSD_GROUPS = 8
SSD_STATE = 128
SSD_CONV = 4
SSD_GROUP_HEADS = SSD_HEADS // SSD_GROUPS
SSD_GROUP_COLS = SSD_INNER // SSD_GROUPS

MLA_HEADS = 64
MLA_Q_RANK = 1536
MLA_KV_RANK = 512
MLA_NOPE = 128
MLA_ROPE = 64
MLA_V = 128
ROPE_THETA = 10000.0
ROPE_FREQS = 16

MOE_GROUPS = 8
MOE_PER_GROUP = 8
N_EXPERTS = 64
MOE_TOP_K = 2
MOE_FF = 512

N_PROMPT = BATCH * SEQ
N_SAMPLE = DEC_BATCH * DEC_SEQ
N_TOK = N_PROMPT + N_SAMPLE
N_MOD = 6

V7X_VMEM_LIMIT = 56 * 1024 * 1024
V7X_LANES = 128
MOE_BLOCK = 128
MOE_FF_SPLIT = 2
ROUTER_LANES = V7X_LANES
MLSTM_CHUNK = 256
SSD_CHUNK = 128
ATTN_Q_TILE = 512


def _params(*sem):
    return pltpu.CompilerParams(dimension_semantics=sem, vmem_limit_bytes=V7X_VMEM_LIMIT)


def _row_group(tile, rows_per_tile):
    first_sample = N_PROMPT // rows_per_tile
    per_batch = DEC_SEQ // rows_per_tile
    return jnp.where(tile < first_sample, 0, 1 + (tile - first_sample) // per_batch)


def _any_spec():
    return pl.BlockSpec(memory_space=pl.ANY)


def _chunk_rows(c, length):
    if isinstance(c, int):
        return pl.ds(c * length, length)
    return pl.ds(pl.multiple_of(c * length, length), length)


def _mm_kernel(*refs, has_res):
    if has_res:
        x_ref, w_ref, res_ref, gate_ref, o_ref, wbf_ref = refs
    else:
        x_ref, w_ref, o_ref, wbf_ref = refs

    @pl.when(pl.program_id(1) == 0)
    def _():
        wbf_ref[...] = w_ref[...].astype(jnp.bfloat16)

    acc = jnp.dot(x_ref[...], wbf_ref[...], preferred_element_type=jnp.float32)
    if has_res:
        acc = res_ref[...] + gate_ref[...] * acc
    o_ref[...] = acc.astype(o_ref.dtype)


def _matmul(x, w, *, layer=None, col0=0, n=None, tm=1024, tn=512, out_dtype=jnp.bfloat16,
            res=None, mods=None, gate_chunk=None):
    m, k = x.shape
    if w.ndim == 2:
        w = w[None]
        layer = 0
    n = w.shape[2] - col0 if n is None else n
    tm = min(tm, m)
    assert m % tm == 0 and col0 % tn == 0 and w.shape[1] == k
    cb0 = col0 // tn
    grid = (pl.cdiv(n, tn), m // tm)
    in_specs = [
        pl.BlockSpec((tm, k), lambda j, i: (i, 0)),
        pl.BlockSpec((None, k, tn), lambda j, i: (layer, 0, cb0 + j)),
    ]
    args = [x, w]
    has_res = res is not None
    if has_res:
        in_specs += [
            pl.BlockSpec((tm, tn), lambda j, i: (i, j)),
            pl.BlockSpec((None, None, 1, tn), lambda j, i: (_row_group(i, tm), gate_chunk, 0, j)),
        ]
        args += [res, mods]
    n_out = grid[0] * tn
    return pl.pallas_call(
        functools.partial(_mm_kernel, has_res=has_res),
        name=f"mm_k{k}_n{n}",
        grid=grid,
        in_specs=in_specs,
        out_specs=pl.BlockSpec((tm, tn), lambda j, i: (i, j)),
        out_shape=jax.ShapeDtypeStruct((m, n_out), out_dtype),
        scratch_shapes=[pltpu.VMEM((k, tn), jnp.bfloat16)],
        compiler_params=_params("parallel", "arbitrary"),
    )(*args)


def _norm_kernel(*refs, has_mod, has_router):
    it = iter(refs)
    x_ref, w_ref = next(it), next(it)
    shift_ref = scale_ref = wr_ref = None
    if has_mod:
        shift_ref, scale_ref = next(it), next(it)
    if has_router:
        wr_ref = next(it)
    o_ref = next(it)
    x = x_ref[...].astype(jnp.float32)
    y = x * lax.rsqrt(jnp.mean(x * x, axis=-1, keepdims=True) + EPS) * w_ref[...]
    if has_mod:
        y = y * (1.0 + scale_ref[...]) + shift_ref[...]
    o_ref[...] = y.astype(o_ref.dtype)
    if has_router:
        logits_ref = next(it)
        logits_ref[...] = jnp.dot(y, wr_ref[...], preferred_element_type=jnp.float32,
                                  precision=lax.Precision.HIGHEST)


def _norm(x, w, *, mods=None, shift_chunk=None, scale_chunk=None, router_w=None, tm=256,
          out_dtype=jnp.bfloat16):
    m, d = x.shape
    has_mod = mods is not None
    has_router = router_w is not None
    in_specs = [pl.BlockSpec((tm, d), lambda i: (i, 0)), pl.BlockSpec((1, d), lambda i: (0, 0))]
    args = [x, w.reshape(1, d)]
    if has_mod:
        in_specs += [
            pl.BlockSpec((None, None, 1, d), lambda i: (_row_group(i, tm), shift_chunk, 0, 0)),
            pl.BlockSpec((None, None, 1, d), lambda i: (_row_group(i, tm), scale_chunk, 0, 0)),
        ]
        args += [mods, mods]
    out_specs = pl.BlockSpec((tm, d), lambda i: (i, 0))
    out_shape = jax.ShapeDtypeStruct((m, d), out_dtype)
    if has_router:
        in_specs.append(pl.BlockSpec((d, ROUTER_LANES), lambda i: (0, 0)))
        args.append(router_w)
        out_specs = [out_specs, pl.BlockSpec((tm, ROUTER_LANES), lambda i: (i, 0))]
        out_shape = [out_shape, jax.ShapeDtypeStruct((m, ROUTER_LANES), jnp.float32)]
    return pl.pallas_call(
        functools.partial(_norm_kernel, has_mod=has_mod, has_router=has_router),
        name="norm_router" if has_router else "norm",
        grid=(m // tm,),
        in_specs=in_specs,
        out_specs=out_specs,
        out_shape=out_shape,
        compiler_params=_params("parallel"),
    )(*args)


def _resid_kernel(x_ref, y_ref, gate_ref, o_ref):
    o_ref[...] = x_ref[...] + gate_ref[...] * y_ref[...].astype(jnp.float32)


def _gated_residual(x, y, mods, gate_chunk, tm=256):
    m, d = x.shape
    return pl.pallas_call(
        _resid_kernel,
        name="gated_residual",
        grid=(m // tm,),
        in_specs=[
            pl.BlockSpec((tm, d), lambda i: (i, 0)),
            pl.BlockSpec((tm, d), lambda i: (i, 0)),
            pl.BlockSpec((None, None, 1, d), lambda i: (_row_group(i, tm), gate_chunk, 0, 0)),
        ],
        out_specs=pl.BlockSpec((tm, d), lambda i: (i, 0)),
        out_shape=jax.ShapeDtypeStruct((m, d), jnp.float32),
        compiler_params=_params("parallel"),
    )(x, y, mods)


def _moe_kernel(be_ref, first_ref, valid_ref, x_ref, wg_ref, wu_ref, wd_ref, o_ref, wg_bf, wu_bf, wd_bf):
    i = pl.program_id(1)

    @pl.when(first_ref[i] == 1)
    def _():
        wg_bf[...] = wg_ref[...].astype(jnp.bfloat16)
        wu_bf[...] = wu_ref[...].astype(jnp.bfloat16)
        wd_bf[...] = wd_ref[...].astype(jnp.bfloat16)

    @pl.when(valid_ref[i] == 1)
    def _():
        x = x_ref[...]
        g = jnp.dot(x, wg_bf[...], preferred_element_type=jnp.float32)
        u = jnp.dot(x, wu_bf[...], preferred_element_type=jnp.float32)
        a = (g * jax.nn.sigmoid(g) * u).astype(jnp.bfloat16)
        o_ref[...] = jnp.dot(a, wd_bf[...], preferred_element_type=jnp.float32)

    @pl.when(valid_ref[i] == 0)
    def _():
        o_ref[...] = jnp.zeros_like(o_ref)


def _moe_ffn(xg, block_expert, block_first, block_valid, w_gate, w_up, w_down, layer):
    n_rows, d = xg.shape
    n_blocks = n_rows // MOE_BLOCK
    ffs = MOE_FF // MOE_FF_SPLIT
    grid_spec = pltpu.PrefetchScalarGridSpec(
        num_scalar_prefetch=3,
        grid=(MOE_FF_SPLIT, n_blocks),
        in_specs=[
            pl.BlockSpec((MOE_BLOCK, d), lambda j, i, be, bf, bv: (i, 0)),
            pl.BlockSpec((None, None, d, ffs), lambda j, i, be, bf, bv: (layer, be[i], 0, j)),
            pl.BlockSpec((None, None, d, ffs), lambda j, i, be, bf, bv: (layer, be[i], 0, j)),
            pl.BlockSpec((None, None, ffs, d), lambda j, i, be, bf, bv: (layer, be[i], j, 0)),
        ],
        out_specs=pl.BlockSpec((None, MOE_BLOCK, d), lambda j, i, be, bf, bv: (j, i, 0)),
        scratch_shapes=[
            pltpu.VMEM((d, ffs), jnp.bfloat16),
            pltpu.VMEM((d, ffs), jnp.bfloat16),
            pltpu.VMEM((ffs, d), jnp.bfloat16),
        ],
    )
    return pl.pallas_call(
        _moe_kernel,
        name="moe_ffn",
        grid_spec=grid_spec,
        out_shape=jax.ShapeDtypeStruct((MOE_FF_SPLIT, n_rows, d), jnp.float32),
        compiler_params=_params("arbitrary", "arbitrary"),
    )(block_expert, block_first, block_valid, xg, w_gate, w_up, w_down)


def _route(logits):
    t = logits.shape[0]
    g_logits = logits[:, :MOE_GROUPS]
    g_sel = jnp.argmax(g_logits, axis=-1)
    g_prob = jnp.take_along_axis(jax.nn.softmax(g_logits, axis=-1), g_sel[:, None], axis=1)
    e_logits = logits[:, MOE_GROUPS:MOE_GROUPS + N_EXPERTS].reshape(t, MOE_GROUPS, MOE_PER_GROUP)
    e_in_group = jnp.take_along_axis(e_logits, g_sel[:, None, None], axis=1)[:, 0]
    top_val, top_idx = lax.top_k(e_in_group, MOE_TOP_K)
    gates = g_prob * jax.nn.softmax(top_val, axis=-1)
    expert = (g_sel[:, None] * MOE_PER_GROUP + top_idx).astype(jnp.int32)
    return gates, expert


def _moe_layout(expert):
    n_assign = expert.size
    e_flat = expert.reshape(-1)
    onehot = (e_flat[:, None] == jnp.arange(N_EXPERTS, dtype=jnp.int32)[None, :]).astype(jnp.int32)
    csum = jnp.cumsum(onehot, axis=0)
    rank = jnp.take_along_axis(csum, e_flat[:, None], axis=1)[:, 0] - 1
    counts = csum[-1]
    padded = (counts + MOE_BLOCK - 1) // MOE_BLOCK * MOE_BLOCK
    p_ends = jnp.cumsum(padded)
    p_starts = p_ends - padded
    pos = (p_starts[e_flat] + rank).astype(jnp.int32)
    n_blocks = -(-n_assign // MOE_BLOCK) + N_EXPERTS
    n_rows = n_blocks * MOE_BLOCK
    token = jnp.repeat(jnp.arange(n_assign // MOE_TOP_K, dtype=jnp.int32), MOE_TOP_K)
    row_token = jnp.zeros((n_rows,), jnp.int32).at[pos].set(token)
    block_start = jnp.arange(n_blocks, dtype=jnp.int32) * MOE_BLOCK
    block_expert = jnp.minimum(jnp.searchsorted(p_ends, block_start, side='right'),
                               N_EXPERTS - 1).astype(jnp.int32)
    block_valid = (block_start < p_ends[-1]).astype(jnp.int32)
    prev = jnp.concatenate([jnp.full((1,), -1, jnp.int32), block_expert[:-1]])
    block_first = (block_expert != prev).astype(jnp.int32)
    return pos.reshape(expert.shape), row_token, block_expert, block_first, block_valid


def _hier_moe(h, logits, w_gate, w_up, w_down, layer):
    gates, expert = _route(logits)
    pos, row_token, block_expert, block_first, block_valid = _moe_layout(expert)
    xg = jnp.take(h, row_token, axis=0)
    parts = _moe_ffn(xg, block_expert, block_first, block_valid, w_gate, w_up, w_down, layer)
    yb = parts[0] + parts[1]
    return (jnp.take(yb, pos[:, 0], axis=0) * gates[:, 0:1]
            + jnp.take(yb, pos[:, 1], axis=0) * gates[:, 1:2])


def _log_sigmoid(x):
    return jnp.minimum(x, 0.0) - jnp.log(1.0 + jnp.exp(-jnp.abs(x)))


def _softplus(x):
    return jnp.maximum(x, 0.0) + jnp.log(1.0 + jnp.exp(-jnp.abs(x)))


def _mlstm_kernel(*refs, seq, has_init, emit_state, n_alias):
    f32, bf16 = jnp.float32, jnp.bfloat16
    it = iter(refs)
    q_ref, k_ref, v_ref, o_ref, gc_ref, gr_ref, nw_ref = (next(it) for _ in range(7))
    if has_init:
        c0_ref, n0_ref, m0_ref = next(it), next(it), next(it)
    for _ in range(n_alias):
        next(it)
    y_ref = next(it)
    if emit_state:
        cout_ref, nout_ref, mout_ref = next(it), next(it), next(it)
    hacc, c_sc, n_sc, m_sc = next(it), next(it), next(it), next(it)

    L = MLSTM_CHUNK
    nc = seq // L
    use_inter = has_init or nc > 1
    scale = MLSTM_DQK ** -0.5
    row = lax.broadcasted_iota(jnp.int32, (L, L), 0)
    col = lax.broadcasted_iota(jnp.int32, (L, L), 1)

    for d in range(2):
        tri = (col <= row) if d == 0 else (col >= row)
        tri_t = (row <= col) if d == 0 else (row >= col)
        if has_init:
            c_sc[...] = c0_ref[d]
            n_sc[...] = n0_ref[d]
            m_sc[...] = m0_ref[d]
        else:
            c_sc[...] = jnp.zeros_like(c_sc)
            n_sc[...] = jnp.zeros_like(n_sc)
            m_sc[...] = jnp.zeros_like(m_sc)

        def chunk(ci, carry, d=d, tri=tri, tri_t=tri_t):
            c = ci if d == 0 else nc - 1 - ci
            rows = _chunk_rows(c, L)
            q, k, v = q_ref[rows, :], k_ref[rows, :], v_ref[rows, :]
            gcol, grow = gc_ref[c], gr_ref[c]
            i_col = gcol[:, 2 * d:2 * d + 1]
            f_col = _log_sigmoid(gcol[:, 2 * d + 1:2 * d + 2])
            i_row = grow[2 * d:2 * d + 1, :]
            f_row = _log_sigmoid(grow[2 * d + 1:2 * d + 2, :])
            b_col = jnp.sum(jnp.where(tri, f_row, 0.0), axis=1, keepdims=True)
            b_row = jnp.sum(jnp.where(tri_t, f_col, 0.0), axis=0, keepdims=True)
            b_end = jnp.sum(f_row, axis=1, keepdims=True)
            m_prev = m_sc[...]
            log_d = jnp.where(tri, b_col - b_row + i_row, -jnp.inf)
            log_inter = b_col + m_prev
            m_t = jnp.maximum(log_inter, jnp.max(log_d, axis=1, keepdims=True))
            w_intra = jnp.exp(log_d - m_t)
            qk = lax.dot_general(q, k, (((1,), (1,)), ((), ())), preferred_element_type=f32) * (scale * w_intra)
            num = jnp.dot(qk.astype(bf16), v, preferred_element_type=f32)
            den = jnp.sum(qk, axis=1, keepdims=True)
            if use_inter:
                w_inter = jnp.exp(log_inter - m_t) * scale
                num = num + w_inter * jnp.dot(q, c_sc[...].astype(bf16), preferred_element_type=f32)
                den = den + w_inter * jnp.sum(q.astype(f32) * n_sc[...], axis=1, keepdims=True)
            h = num / jnp.maximum(jnp.abs(den), jnp.exp(-m_t))
            if d == 0:
                hacc[rows, :] = h
            else:
                hacc[rows, :] = hacc[rows, :] + h
            if emit_state or nc > 1:
                log_w = b_end - b_col + i_col
                m_new = jnp.maximum(b_end + m_prev, jnp.max(log_w, axis=0, keepdims=True))
                kw = k.astype(f32) * jnp.exp(log_w - m_new)
                decay = jnp.exp(b_end + m_prev - m_new)
                c_sc[...] = decay * c_sc[...] + jnp.dot(kw.T.astype(bf16), v, preferred_element_type=f32)
                n_sc[...] = decay * n_sc[...] + jnp.sum(kw, axis=0, keepdims=True)
                m_sc[...] = m_new
            return carry

        if nc == 1:
            chunk(0, 0)
        else:
            lax.fori_loop(0, nc, chunk, 0)
        if emit_state:
            cout_ref[d] = c_sc[...]
            nout_ref[d] = n_sc[...]
            mout_ref[d] = m_sc[...]

    hs = hacc[...]
    hn = hs * lax.rsqrt(jnp.mean(hs * hs, axis=-1, keepdims=True) + EPS) * nw_ref[...]
    y_ref[...] = (hn * jax.nn.sigmoid(o_ref[...].astype(f32))).astype(y_ref.dtype)


def _mlstm_seq(proj, gates, norm_w, layer, *, row0, nb, seq, y_prev=None, init=None, state_prev=None,
               emit_state=False, n_layers=1):
    f32 = jnp.float32
    m_tot = proj.shape[0]
    H, L = MLSTM_HEADS, MLSTM_CHUNK
    nc = seq // L
    rb0 = row0 // seq
    g = gates[row0:row0 + nb * seq].reshape(nb, nc, L, 2, 2, H)
    g_col = g.transpose(0, 5, 1, 2, 3, 4).reshape(nb, H, nc, L, 4)
    g_row = g.transpose(0, 5, 1, 3, 4, 2).reshape(nb, H, nc, 4, L)
    qb, vb = MLSTM_DQK, MLSTM_DV
    in_specs = [
        pl.BlockSpec((seq, qb), lambda b, h: (rb0 + b, h)),
        pl.BlockSpec((seq, qb), lambda b, h: (rb0 + b, H + h)),
        pl.BlockSpec((seq, vb), lambda b, h: (rb0 + b, H + h)),
        pl.BlockSpec((seq, vb), lambda b, h: (rb0 + b, 2 * H + h)),
        pl.BlockSpec((None, None, nc, L, 4), lambda b, h: (b, h, 0, 0, 0)),
        pl.BlockSpec((None, None, nc, 4, L), lambda b, h: (b, h, 0, 0, 0)),
        pl.BlockSpec((None, None, 1, vb), lambda b, h: (layer, h, 0, 0)),
    ]
    args = [proj, proj, proj, proj, g_col, g_row, norm_w.reshape(-1, H, 1, vb)]
    has_init = init is not None
    if has_init:
        c0, n0, m0 = init
        in_specs += [
            pl.BlockSpec((None, None, 2, None, qb, vb), lambda b, h: (b, layer, 0, h, 0, 0)),
            pl.BlockSpec((None, None, 2, None, 1, qb), lambda b, h: (b, layer, 0, h, 0, 0)),
            pl.BlockSpec((None, None, 2, None, 1, 1), lambda b, h: (b, layer, 0, h, 0, 0)),
        ]
        args += [c0, n0.reshape(*n0.shape[:4], 1, qb), m0.reshape(*m0.shape[:4], 1, 1)]
    out_specs = [pl.BlockSpec((seq, vb), lambda b, h: (rb0 + b, h))]
    out_shape = [jax.ShapeDtypeStruct((m_tot, H * vb), jnp.bfloat16)]
    if emit_state:
        out_specs += [
            pl.BlockSpec((None, None, 2, None, qb, vb), lambda b, h: (b, layer, 0, h, 0, 0)),
            pl.BlockSpec((None, None, 2, None, 1, qb), lambda b, h: (b, layer, 0, h, 0, 0)),
            pl.BlockSpec((None, None, 2, None, 1, 1), lambda b, h: (b, layer, 0, h, 0, 0)),
        ]
        out_shape += [
            jax.ShapeDtypeStruct((nb, n_layers, 2, H, qb, vb), f32),
            jax.ShapeDtypeStruct((nb, n_layers, 2, H, 1, qb), f32),
            jax.ShapeDtypeStruct((nb, n_layers, 2, H, 1, 1), f32),
        ]
    aliases = {}
    n_alias = 0
    if y_prev is not None:
        aliases[len(args)] = 0
        in_specs.append(_any_spec())
        args.append(y_prev)
        n_alias += 1
    if state_prev is not None:
        for k_out, a in enumerate(state_prev):
            aliases[len(args)] = 1 + k_out
            in_specs.append(_any_spec())
            args.append(a)
            n_alias += 1
    return pl.pallas_call(
        functools.partial(_mlstm_kernel, seq=seq, has_init=has_init, emit_state=emit_state, n_alias=n_alias),
        name=f"mlstm_s{seq}",
        grid=(nb, H),
        in_specs=in_specs,
        out_specs=out_specs,
        out_shape=out_shape,
        input_output_aliases=aliases,
        scratch_shapes=[
            pltpu.VMEM((seq, vb), f32),
            pltpu.VMEM((qb, vb), f32),
            pltpu.VMEM((1, qb), f32),
            pltpu.VMEM((1, 1), f32),
        ],
        compiler_params=_params("arbitrary", "arbitrary"),
    )(*args)


def _conv_silu(x, w_ref, b_ref):
    s = x.shape[0]
    t = lax.broadcasted_iota(jnp.int32, x.shape, 0)
    y = (w_ref[0:1, :] * jnp.where(t >= 2, pltpu.roll(x, 2, 0), 0.0)
         + w_ref[1:2, :] * jnp.where(t >= 1, pltpu.roll(x, 1, 0), 0.0)
         + w_ref[2:3, :] * x
         + w_ref[3:4, :] * jnp.where(t < s - 1, pltpu.roll(x, s - 1, 0), 0.0)
         + b_ref[...])
    return y * jax.nn.sigmoid(y)


def _ssd_kernel(*refs, seq, has_init, emit_state, n_alias):
    f32, bf16 = jnp.float32, jnp.bfloat16
    it = iter(refs)
    (x_ref, b_ref, c_ref, z_ref, dtc_ref, dtr_ref, dtbc_ref, dtbr_ref, alc_ref, alr_ref,
     cwx_ref, cbx_ref, cwb_ref, cbb_ref, cwc_ref, cbc_ref, dsk_ref, nw_ref) = (next(it) for _ in range(18))
    if has_init:
        h0_ref = next(it)
    for _ in range(n_alias):
        next(it)
    y_ref = next(it)
    if emit_state:
        hout_ref = next(it)
    xc_sc, bc_sc, cc_sc, yacc, hst = (next(it) for _ in range(5))

    L = SSD_CHUNK
    nc = seq // L
    E, P = SSD_GROUP_HEADS, SSD_HEAD_DIM
    n_pair = E // 2
    W = 2 * P

    for p in range(n_pair):
        cols = slice(p * W, (p + 1) * W)
        xc_sc[:, cols] = _conv_silu(x_ref[:, cols].astype(f32), cwx_ref.at[:, cols], cbx_ref.at[:, cols])
    bc_sc[...] = _conv_silu(b_ref[...].astype(f32), cwb_ref, cbb_ref).astype(bf16)
    cc_sc[...] = _conv_silu(c_ref[...].astype(f32), cwc_ref, cbc_ref).astype(bf16)

    row = lax.broadcasted_iota(jnp.int32, (L, L), 0)
    col = lax.broadcasted_iota(jnp.int32, (L, L), 1)
    lo = lax.broadcasted_iota(jnp.int32, (L, W), 1) < P
    lo_rows = lax.broadcasted_iota(jnp.int32, (W, SSD_STATE), 0) < P
    a_col = -jnp.exp(alc_ref[...])
    a_row = -jnp.exp(alr_ref[...])

    for d in range(2):
        tri = (col <= row) if d == 0 else (col >= row)
        tri_f = tri.astype(f32)
        tri_tf = ((row <= col) if d == 0 else (row >= col)).astype(f32)
        if has_init:
            hst[...] = h0_ref[d]
        else:
            hst[...] = jnp.zeros_like(hst)

        def chunk(ci, carry, d=d, tri=tri, tri_f=tri_f, tri_tf=tri_tf):
            c = ci if d == 0 else nc - 1 - ci
            rows = _chunk_rows(c, L)
            dt_c = _softplus(dtc_ref[c] + dtbc_ref[...])
            dt_r = _softplus(dtr_ref[c] + dtbr_ref[...])
            cs_c = jnp.dot(tri_f, dt_c * a_col, preferred_element_type=f32, precision=lax.Precision.HIGHEST)
            cs_r = jnp.dot(dt_r * a_row, tri_tf, preferred_element_type=f32, precision=lax.Precision.HIGHEST)
            cs_e = cs_c[L - 1:L, :] if d == 0 else cs_c[0:1, :]
            bm, cm = bc_sc[rows, :], cc_sc[rows, :]
            cb = lax.dot_general(cm, bm, (((1,), (1,)), ((), ())), preferred_element_type=f32)
            for p in range(n_pair):
                cols = slice(p * W, (p + 1) * W)
                ia, ib = d * E + 2 * p, d * E + 2 * p + 1
                xdt = xc_sc[rows, cols] * jnp.where(lo, dt_c[:, ia:ia + 1], dt_c[:, ib:ib + 1])
                xdt_bf = xdt.astype(bf16)
                cs_ta, cs_tb = cs_c[:, ia:ia + 1], cs_c[:, ib:ib + 1]
                wa = (cb * jnp.exp(jnp.where(tri, cs_ta - cs_r[ia:ia + 1, :], -jnp.inf))).astype(bf16)
                wb = (cb * jnp.exp(jnp.where(tri, cs_tb - cs_r[ib:ib + 1, :], -jnp.inf))).astype(bf16)
                y = jnp.where(lo, jnp.dot(wa, xdt_bf, preferred_element_type=f32),
                              jnp.dot(wb, xdt_bf, preferred_element_type=f32))
                hrows = slice(p * W, (p + 1) * W)
                hp = hst[hrows, :]
                if has_init or nc > 1:
                    y_inter = lax.dot_general(cm, hp.astype(bf16), (((1,), (1,)), ((), ())),
                                              preferred_element_type=f32)
                    y = y + jnp.where(lo, jnp.exp(cs_ta), jnp.exp(cs_tb)) * y_inter
                if d == 0:
                    yacc[rows, cols] = y
                else:
                    yacc[rows, cols] = yacc[rows, cols] + y
                if emit_state or nc > 1:
                    ea, eb = cs_e[:, ia:ia + 1], cs_e[:, ib:ib + 1]
                    xw = xdt * jnp.where(lo, jnp.exp(ea - cs_ta), jnp.exp(eb - cs_tb))
                    contrib = jnp.dot(xw.T.astype(bf16), bm, preferred_element_type=f32)
                    hst[hrows, :] = jnp.where(lo_rows, jnp.exp(ea), jnp.exp(eb)) * hp + contrib
            return carry

        if nc == 1:
            chunk(0, 0)
        else:
            lax.fori_loop(0, nc, chunk, 0)
        if emit_state:
            hout_ref[d] = hst[...]

    y = yacc[...] + dsk_ref[...] * xc_sc[...]
    z = z_ref[...].astype(f32)
    y = y * (z * jax.nn.sigmoid(z))
    y = y * lax.rsqrt(jnp.mean(y * y, axis=-1, keepdims=True) + EPS) * nw_ref[...]
    y_ref[...] = y.astype(y_ref.dtype)


def _ssd_seq(zx, dt_raw, conv_w, conv_b, dt_bias, a_log, d_skip, norm_w, layer, *, row0, nb, seq,
             y_prev=None, init=None, emit_state=False):
    f32 = jnp.float32
    m_tot = zx.shape[0]
    G, E, L, N = SSD_GROUPS, SSD_GROUP_HEADS, SSD_CHUNK, SSD_STATE
    gc = SSD_GROUP_COLS
    nc = seq // L
    rb0 = row0 // seq
    dt = dt_raw[row0:row0 + nb * seq].reshape(nb, nc, L, 2, G, E)
    dt_col = dt.transpose(0, 4, 1, 2, 3, 5).reshape(nb, G, nc, L, 2 * E)
    dt_row = dt.transpose(0, 4, 1, 3, 5, 2).reshape(nb, G, nc, 2 * E, L)

    def per_group(a):
        a = a.astype(f32).reshape(2, G, E).transpose(1, 0, 2).reshape(G, 2 * E)
        return a[:, None, :], a[:, :, None]

    dtb_c, dtb_r = per_group(dt_bias)
    al_c, al_r = per_group(a_log)
    x_cb = SSD_INNER // gc
    bc0 = 2 * SSD_INNER // N
    cw = conv_w.astype(f32)
    cbias = conv_b.astype(f32).reshape(1, -1)
    dsk = jnp.repeat(d_skip.astype(f32), SSD_HEAD_DIM).reshape(1, SSD_INNER)
    in_specs = [
        pl.BlockSpec((seq, gc), lambda b, g: (rb0 + b, x_cb + g)),
        pl.BlockSpec((seq, N), lambda b, g: (rb0 + b, bc0 + g)),
        pl.BlockSpec((seq, N), lambda b, g: (rb0 + b, bc0 + G + g)),
        pl.BlockSpec((seq, gc), lambda b, g: (rb0 + b, g)),
        pl.BlockSpec((None, None, nc, L, 2 * E), lambda b, g: (b, g, 0, 0, 0)),
        pl.BlockSpec((None, None, nc, 2 * E, L), lambda b, g: (b, g, 0, 0, 0)),
        pl.BlockSpec((None, 1, 2 * E), lambda b, g: (g, 0, 0)),
        pl.BlockSpec((None, 2 * E, 1), lambda b, g: (g, 0, 0)),
        pl.BlockSpec((None, 1, 2 * E), lambda b, g: (g, 0, 0)),
        pl.BlockSpec((None, 2 * E, 1), lambda b, g: (g, 0, 0)),
        pl.BlockSpec((SSD_CONV, gc), lambda b, g: (0, g)),
        pl.BlockSpec((1, gc), lambda b, g: (0, g)),
        pl.BlockSpec((SSD_CONV, N), lambda b, g: (0, SSD_INNER // N + g)),
        pl.BlockSpec((1, N), lambda b, g: (0, SSD_INNER // N + g)),
        pl.BlockSpec((SSD_CONV, N), lambda b, g: (0, SSD_INNER // N + G + g)),
        pl.BlockSpec((1, N), lambda b, g: (0, SSD_INNER // N + G + g)),
        pl.BlockSpec((1, gc), lambda b, g: (0, g)),
        pl.BlockSpec((1, gc), lambda b, g: (0, g)),
    ]
    args = [zx, zx, zx, zx, dt_col, dt_row, dtb_c, dtb_r, al_c, al_r,
            cw, cbias, cw, cbias, cw, cbias, dsk, norm_w.astype(f32).reshape(1, SSD_INNER)]
    has_init = init is not None
    state_block = pl.BlockSpec((None, None, 2, None, E * SSD_HEAD_DIM, N), lambda b, g: (b, layer, 0, g, 0, 0))
    if has_init:
        in_specs.append(state_block)
        args.append(init.reshape(*init.shape[:3], G, E * SSD_HEAD_DIM, N))
    out_specs = [pl.BlockSpec((seq, gc), lambda b, g: (rb0 + b, g))]
    out_shape = [jax.ShapeDtypeStruct((m_tot, SSD_INNER), jnp.bfloat16)]
    if emit_state:
        out_specs.append(state_block)
        out_shape.append(jax.ShapeDtypeStruct((nb, 1, 2, G, E * SSD_HEAD_DIM, N), f32))
    aliases = {}
    n_alias = 0
    if y_prev is not None:
        aliases[len(args)] = 0
        in_specs.append(_any_spec())
        args.append(y_prev)
        n_alias = 1
    return pl.pallas_call(
        functools.partial(_ssd_kernel, seq=seq, has_init=has_init, emit_state=emit_state, n_alias=n_alias),
        name=f"ssd_s{seq}",
        grid=(nb, G),
        in_specs=in_specs,
        out_specs=out_specs,
        out_shape=out_shape,
        input_output_aliases=aliases,
        scratch_shapes=[
            pltpu.VMEM((seq, gc), f32),
            pltpu.VMEM((seq, N), jnp.bfloat16),
            pltpu.VMEM((seq, N), jnp.bfloat16),
            pltpu.VMEM((seq, gc), f32),
            pltpu.VMEM((E * SSD_HEAD_DIM, N), f32),
        ],
        compiler_params=_params("arbitrary", "arbitrary"),
    )(*args)


def _attn_kernel(*refs, heads, rope):
    f32, bf16 = jnp.float32, jnp.bfloat16
    if rope:
        qn_ref, qp_ref, kv_ref, kpe_ref, cos_ref, sin_ref, o_ref = refs
    else:
        qn_ref, qp_ref, kv_ref, kpe_ref, o_ref = refs
    scale = (MLA_NOPE + MLA_ROPE) ** -0.5
    kpe2 = kpe_ref[...]
    lq = qn_ref.shape[0]
    lane = lax.broadcasted_iota(jnp.int32, (lq, V7X_LANES), 1)
    nt = (((1,), (1,)), ((), ()))
    for pr in range(heads // 2):
        qp = qp_ref[:, pr * V7X_LANES:(pr + 1) * V7X_LANES]
        if rope:
            x = qp.astype(f32)
            other = jnp.where(lane % (2 * ROPE_FREQS) < ROPE_FREQS,
                              pltpu.roll(x, V7X_LANES - ROPE_FREQS, 1), pltpu.roll(x, ROPE_FREQS, 1))
            qp = (x * cos_ref[...] + other * sin_ref[...]).astype(bf16)
        for a in range(2):
            h = 2 * pr + a
            qpa = jnp.where((lane < MLA_ROPE) if a == 0 else (lane >= MLA_ROPE), qp, jnp.zeros_like(qp))
            kn = kv_ref[:, h * 2 * MLA_NOPE:h * 2 * MLA_NOPE + MLA_NOPE]
            vv = kv_ref[:, h * 2 * MLA_NOPE + MLA_NOPE:(h + 1) * 2 * MLA_NOPE]
            s = (lax.dot_general(qn_ref[:, h * MLA_NOPE:(h + 1) * MLA_NOPE], kn, nt, preferred_element_type=f32)
                 + lax.dot_general(qpa, kpe2, nt, preferred_element_type=f32)) * scale
            e = jnp.exp(s - jnp.max(s, axis=-1, keepdims=True))
            p = e / jnp.sum(e, axis=-1, keepdims=True)
            o_ref[:, h * MLA_V:(h + 1) * MLA_V] = jnp.dot(p.astype(bf16), vv,
                                                          preferred_element_type=f32).astype(o_ref.dtype)


def _attention(qn, qp, kv, kpe2, *, row0, nb, seq, key0, n_keys, q_tile, heads, tables=None, o_prev=None):
    m_tot = qn.shape[0]
    nq = seq // q_tile
    qb0 = row0 // q_tile
    kb0 = key0 // n_keys
    assert row0 % q_tile == 0 and key0 % n_keys == 0
    hb = MLA_HEADS // heads
    in_specs = [
        pl.BlockSpec((q_tile, heads * MLA_NOPE), lambda b, t, h: (qb0 + b * nq + t, h)),
        pl.BlockSpec((q_tile, heads * MLA_ROPE), lambda b, t, h: (qb0 + b * nq + t, h)),
        pl.BlockSpec((n_keys, heads * 2 * MLA_NOPE), lambda b, t, h: (kb0 + b, h)),
        pl.BlockSpec((n_keys, V7X_LANES), lambda b, t, h: (kb0 + b, 0)),
    ]
    args = [qn, qp, kv, kpe2]
    rope = tables is not None
    if rope:
        in_specs += [pl.BlockSpec((q_tile, V7X_LANES), lambda b, t, h: (t, 0))] * 2
        args += list(tables)
    aliases = {}
    if o_prev is not None:
        aliases[len(args)] = 0
        in_specs.append(_any_spec())
        args.append(o_prev)

    def body(*refs):
        if o_prev is not None:
            refs = refs[:len(args) - 1] + refs[len(args):]
        _attn_kernel(*refs, heads=heads, rope=rope)

    return pl.pallas_call(
        body,
        name=f"attn_s{seq}",
        grid=(nb, nq, hb),
        in_specs=in_specs,
        out_specs=pl.BlockSpec((q_tile, heads * MLA_V), lambda b, t, h: (qb0 + b * nq + t, h)),
        out_shape=jax.ShapeDtypeStruct((m_tot, MLA_HEADS * MLA_V), jnp.bfloat16),
        input_output_aliases=aliases,
        compiler_params=_params("arbitrary", "arbitrary", "arbitrary"),
    )(*args)


def _rope_tables(n_tok):
    rows = n_tok // GRID_W
    row = jnp.repeat(jnp.arange(rows, dtype=jnp.float32), GRID_W)
    col = jnp.tile(jnp.arange(GRID_W, dtype=jnp.float32), rows)
    inv_freq = ROPE_THETA ** (-jnp.arange(ROPE_FREQS, dtype=jnp.float32) / ROPE_FREQS)
    ang = jnp.stack([row, col], axis=-1)[:, :, None] * inv_freq
    cos, sin = jnp.cos(ang), jnp.sin(ang)
    cos64 = jnp.concatenate([cos[:, 0], cos[:, 0], cos[:, 1], cos[:, 1]], axis=-1)
    sin64 = jnp.concatenate([-sin[:, 0], sin[:, 0], -sin[:, 1], sin[:, 1]], axis=-1)
    return jnp.tile(cos64, (1, 2)), jnp.tile(sin64, (1, 2))


def _rope_rows(x, cos64, sin64):
    xr = x.reshape(x.shape[0], 2, 2, ROPE_FREQS)
    other = jnp.stack([xr[:, :, 1], xr[:, :, 0]], axis=2).reshape(x.shape)
    return x * cos64 + other * sin64


def kernel(x_prompt, x_sample, state_mlstm_C, state_mlstm_n, state_mlstm_m, state_ssm, cache_mla_ckv, cache_mla_kpe, c, c_ctx, ada_w, ada_b, norm_mix, norm_ffn, mlstm_w_in, mlstm_b_gates, mlstm_norm, mlstm_w_out, ssd_w_in, ssd_conv_w, ssd_conv_b, ssd_dt_bias, ssd_A_log, ssd_D, ssd_norm, ssd_w_out, mla_w_in, mla_q_norm, mla_kv_norm, mla_w_qb, mla_w_kvb, mla_w_out, router_group_w, router_expert_w, moe_w_gate, moe_w_up, moe_w_down, final_norm):
    f32, bf16 = jnp.float32, jnp.bfloat16
    D = D_MODEL
    X = jnp.concatenate([x_prompt.reshape(N_PROMPT, D), x_sample.reshape(N_SAMPLE, D)], axis=0)

    cond = jnp.concatenate([c_ctx[None, :], c], axis=0)
    cond = jnp.pad(jax.nn.silu(cond), ((0, 16 - cond.shape[0]), (0, 0))).astype(bf16)

    n_mlstm = mlstm_w_in.shape[0]
    mlstm_state = None
    new_ssm, new_ckv, new_kpe = [], [], []
    for l in range(DEPTH):
        kind, j = l % N_MIXERS, l // N_MIXERS
        ada = _matmul(cond, ada_w, layer=l, tm=16, tn=512, out_dtype=f32)
        mods = (ada[:1 + DEC_BATCH] + ada_b[l]).reshape(1 + DEC_BATCH, N_MOD, 1, D)
        h = _norm(X, norm_mix[l], mods=mods, shift_chunk=0, scale_chunk=1)

        if kind == 0:
            proj = _matmul(h, mlstm_w_in, layer=j, n=12288)
            gates = _matmul(h, mlstm_w_in, layer=j, col0=12288, n=128, tn=128, out_dtype=f32)
            gates = gates[:, :4 * MLSTM_HEADS] + mlstm_b_gates[j]
            outs = _mlstm_seq(proj, gates, mlstm_norm, j, row0=0, nb=BATCH, seq=SEQ, emit_state=True,
                              state_prev=mlstm_state, n_layers=n_mlstm)
            y, mlstm_state = outs[0], tuple(outs[1:])
            y = _mlstm_seq(proj, gates, mlstm_norm, j, row0=N_PROMPT, nb=DEC_BATCH, seq=DEC_SEQ, y_prev=y,
                           init=(state_mlstm_C, state_mlstm_n, state_mlstm_m))[0]
            X = _matmul(y, mlstm_w_out, layer=j, out_dtype=f32, res=X, mods=mods, gate_chunk=2)
        elif kind == 1:
            zx = _matmul(h, ssd_w_in, layer=j, n=18432)
            dt_raw = _matmul(h, ssd_w_in, layer=j, col0=18432, n=256, tn=256, out_dtype=f32)
            w = (ssd_conv_w[j], ssd_conv_b[j], ssd_dt_bias[j], ssd_A_log[j], ssd_D[j], ssd_norm[j])
            y, s_h = _ssd_seq(zx, dt_raw, *w, 0, row0=0, nb=BATCH, seq=SEQ, emit_state=True)
            y = _ssd_seq(zx, dt_raw, *w, j, row0=N_PROMPT, nb=DEC_BATCH, seq=DEC_SEQ, y_prev=y, init=state_ssm)[0]
            new_ssm.append(s_h.reshape(BATCH, 2, SSD_HEADS, SSD_HEAD_DIM, SSD_STATE))
            X = _matmul(y, ssd_w_out, layer=j, tm=512, tn=256, out_dtype=f32, res=X, mods=mods, gate_chunk=2)
        else:
            lat = _matmul(h, mla_w_in, layer=j, n=2048, out_dtype=f32)
            kpe = _matmul(h, mla_w_in, layer=j, col0=2048, n=128, tn=128, out_dtype=f32)[:, :MLA_ROPE]
            hq = _norm(lat[:, :MLA_Q_RANK], mla_q_norm[j])
            ckv = _norm(lat[:, MLA_Q_RANK:], mla_kv_norm[j], out_dtype=f32)
            wq = mla_w_qb[j].reshape(MLA_Q_RANK, MLA_HEADS, MLA_NOPE + MLA_ROPE)
            qn = _matmul(hq, wq[:, :, :MLA_NOPE].reshape(MLA_Q_RANK, -1), tn=512)
            qp = _matmul(hq, wq[:, :, MLA_NOPE:].reshape(MLA_Q_RANK, -1), tn=512)
            n_keys = PAST_LEN + DEC_SEQ
            n_skeys = DEC_BATCH * n_keys
            cos2, sin2 = _rope_tables(DEC_SEQ)
            ckv_s = jnp.concatenate([cache_mla_ckv[:, j], ckv[N_PROMPT:].reshape(DEC_BATCH, DEC_SEQ, -1)], axis=1)
            ckv_all = jnp.concatenate([ckv_s.reshape(-1, MLA_KV_RANK), ckv[:N_PROMPT]], axis=0).astype(bf16)
            kv = _matmul(ckv_all, mla_w_kvb, layer=j, tn=512)
            kpe_lat = _rope_rows(kpe[N_PROMPT:], jnp.tile(cos2[:, :MLA_ROPE], (DEC_BATCH, 1)),
                                 jnp.tile(sin2[:, :MLA_ROPE], (DEC_BATCH, 1)))
            kpe_s = jnp.concatenate([cache_mla_kpe[:, j], kpe_lat.reshape(DEC_BATCH, DEC_SEQ, -1)], axis=1)
            kpe_all = jnp.concatenate([kpe_s.reshape(-1, MLA_ROPE), kpe[:N_PROMPT]], axis=0).astype(bf16)
            kpe2 = jnp.concatenate([kpe_all, kpe_all], axis=1)
            o = _attention(qn, qp, kv, kpe2, row0=0, nb=BATCH, seq=SEQ, key0=n_skeys, n_keys=SEQ, q_tile=SEQ,
                           heads=8)
            o = _attention(qn, qp, kv, kpe2, row0=N_PROMPT, nb=DEC_BATCH, seq=DEC_SEQ, key0=0,
                           n_keys=n_keys, q_tile=ATTN_Q_TILE, heads=2, tables=(cos2, sin2), o_prev=o)
            new_ckv.append(ckv[:N_PROMPT].reshape(BATCH, SEQ, MLA_KV_RANK))
            new_kpe.append(kpe[:N_PROMPT].reshape(BATCH, SEQ, MLA_ROPE))
            X = _matmul(o, mla_w_out, layer=j, tm=512, tn=256, out_dtype=f32, res=X, mods=mods, gate_chunk=2)

        router_w = jnp.pad(jnp.concatenate([router_group_w[l], router_expert_w[l]], axis=1),
                           ((0, 0), (0, ROUTER_LANES - MOE_GROUPS - N_EXPERTS)))
        hf, logits = _norm(X, norm_ffn[l], mods=mods, shift_chunk=3, scale_chunk=4, router_w=router_w)
        y = _hier_moe(hf, logits, moe_w_gate, moe_w_up, moe_w_down, l)
        X = _gated_residual(X, y, mods, 5)

    Y = _norm(X, final_norm, out_dtype=f32)
    y_prompt = Y[:N_PROMPT].reshape(BATCH, SEQ, D)
    y_sample = Y[N_PROMPT:].reshape(DEC_BATCH, DEC_SEQ, D)
    new_c, new_n, new_m = mlstm_state
    return (y_prompt, y_sample, new_c,
            new_n.reshape(BATCH, n_mlstm, 2, MLSTM_HEADS, MLSTM_DQK),
            new_m.reshape(BATCH, n_mlstm, 2, MLSTM_HEADS),
            jnp.stack(new_ssm, axis=1), jnp.stack(new_ckv, axis=1), jnp.stack(new_kpe, axis=1))
```

```python
import functools

import jax
import jax.numpy as jnp
from jax import lax
from jax.experimental import pallas as pl
from jax.experimental.pallas import tpu as pltpu

D_MODEL = 4096
BATCH = 32
SEQ = 256
DEPTH = 4
DEC_BATCH = 2
DEC_SEQ = 1024
PAST_LEN = 512
GRID_W = 64
N_MIXERS = 3
EPS = 1e-6

MLSTM_HEADS = 8
MLSTM_DQK = 256
MLSTM_DV = 512

SSD_INNER = 8192
SSD_HEAD_DIM = 64
SSD_HEADS = 128
SSD_GROUPS = 8
SSD_STATE = 128
SSD_CONV = 4
SSD_GROUP_HEADS = SSD_HEADS // SSD_GROUPS
SSD_GROUP_COLS = SSD_INNER // SSD_GROUPS

MLA_HEADS = 64
MLA_Q_RANK = 1536
MLA_KV_RANK = 512
MLA_NOPE = 128
MLA_ROPE = 64
MLA_V = 128
ROPE_THETA = 10000.0
ROPE_FREQS = 16

MOE_GROUPS = 8
MOE_PER_GROUP = 8
N_EXPERTS = 64
MOE_TOP_K = 2
MOE_FF = 512

N_PROMPT = BATCH * SEQ
N_SAMPLE = DEC_BATCH * DEC_SEQ
N_TOK = N_PROMPT + N_SAMPLE
N_MOD = 6

V7X_VMEM_LIMIT = 56 * 1024 * 1024
V7X_LANES = 128
MOE_BLOCK = 256
ROUTER_LANES = V7X_LANES
MLSTM_CHUNK = 256
SSD_CHUNK = 128
ATTN_Q_TILE = 512


def _params(*sem):
    return pltpu.CompilerParams(dimension_semantics=sem, vmem_limit_bytes=V7X_VMEM_LIMIT)


def _row_group(tile, rows_per_tile):
    first_sample = N_PROMPT // rows_per_tile
    per_batch = DEC_SEQ // rows_per_tile
    return jnp.where(tile < first_sample, 0, 1 + (tile - first_sample) // per_batch)


def _any_spec():
    return pl.BlockSpec(memory_space=pl.ANY)


def _chunk_rows(c, length):
    if isinstance(c, int):
        return pl.ds(c * length, length)
    return pl.ds(pl.multiple_of(c * length, length), length)


def _mm_kernel(*refs, has_res, split_tile):
    n_x = 1 if split_tile is None else 2
    x_refs, refs = refs[:n_x], refs[n_x:]
    if has_res:
        w_ref, res_ref, gate_ref, o_ref, wbf_ref = refs
    else:
        w_ref, o_ref, wbf_ref = refs
    i = pl.program_id(1)

    @pl.when(i == 0)
    def _():
        wbf_ref[...] = w_ref[...].astype(jnp.bfloat16)

    def emit(x_ref):
        acc = jnp.dot(x_ref[...], wbf_ref[...], preferred_element_type=jnp.float32)
        if has_res:
            acc = res_ref[...] + gate_ref[...] * acc
        o_ref[...] = acc.astype(o_ref.dtype)

    if split_tile is None:
        emit(x_refs[0])
    else:
        pl.when(i < split_tile)(lambda: emit(x_refs[0]))
        pl.when(i >= split_tile)(lambda: emit(x_refs[1]))


def _matmul(x, w, *, layer=None, col0=0, n=None, tm=1024, tn=512, out_dtype=jnp.bfloat16,
            res=None, mods=None, gate_chunk=None, single_buffer_w=False):
    xs = x if isinstance(x, (tuple, list)) else (x,)
    m, k = sum(a.shape[0] for a in xs), xs[0].shape[1]
    if w.ndim == 2:
        w = w[None]
        layer = 0
    n = w.shape[2] - col0 if n is None else n
    tm = min(tm, m)
    assert all(a.shape[0] % tm == 0 for a in xs) and col0 % tn == 0 and w.shape[1] == k
    cb0 = col0 // tn
    grid = (pl.cdiv(n, tn), m // tm)
    w_mode = {"pipeline_mode": pl.Buffered(1)} if single_buffer_w else {}
    if len(xs) == 1:
        split_tile = None
        in_specs = [pl.BlockSpec((tm, k), lambda j, i: (i, 0))]
    else:
        split_tile = xs[0].shape[0] // tm
        in_specs = [pl.BlockSpec((tm, k), lambda j, i: (jnp.minimum(i, split_tile - 1), 0)),
                    pl.BlockSpec((tm, k), lambda j, i: (jnp.maximum(i - split_tile, 0), 0))]
    in_specs.append(pl.BlockSpec((None, k, tn), lambda j, i: (layer, 0, cb0 + j), **w_mode))
    args = [*xs, w]
    has_res = res is not None
    if has_res:
        in_specs += [
            pl.BlockSpec((tm, tn), lambda j, i: (i, j)),
            pl.BlockSpec((None, None, 1, tn), lambda j, i: (_row_group(i, tm), gate_chunk, 0, j)),
        ]
        args += [res, mods]
    n_out = grid[0] * tn
    return pl.pallas_call(
        functools.partial(_mm_kernel, has_res=has_res, split_tile=split_tile),
        name=f"mm_k{k}_n{n}",
        grid=grid,
        in_specs=in_specs,
        out_specs=pl.BlockSpec((tm, tn), lambda j, i: (i, j)),
        out_shape=jax.ShapeDtypeStruct((m, n_out), out_dtype),
        scratch_shapes=[pltpu.VMEM((k, tn), jnp.bfloat16)],
        compiler_params=_params("parallel", "arbitrary"),
    )(*args)


def _norm_kernel(*refs, has_mod, has_router):
    it = iter(refs)
    x_ref, w_ref = next(it), next(it)
    shift_ref = scale_ref = wr_ref = None
    if has_mod:
        shift_ref, scale_ref = next(it), next(it)
    if has_router:
        wr_ref = next(it)
    o_ref = next(it)
    x = x_ref[...].astype(jnp.float32)
    y = x * lax.rsqrt(jnp.mean(x * x, axis=-1, keepdims=True) + EPS) * w_ref[...]
    if has_mod:
        y = y * (1.0 + scale_ref[...]) + shift_ref[...]
    o_ref[...] = y.astype(o_ref.dtype)
    if has_router:
        logits_ref = next(it)
        logits_ref[...] = jnp.dot(y, wr_ref[...], preferred_element_type=jnp.float32,
                                  precision=lax.Precision.HIGHEST)


def _norm(x, w, *, mods=None, shift_chunk=None, scale_chunk=None, router_w=None, tm=256,
          out_dtype=jnp.bfloat16):
    m, d = x.shape
    has_mod = mods is not None
    has_router = router_w is not None
    in_specs = [pl.BlockSpec((tm, d), lambda i: (i, 0)), pl.BlockSpec((1, d), lambda i: (0, 0))]
    args = [x, w.reshape(1, d)]
    if has_mod:
        in_specs += [
            pl.BlockSpec((None, None, 1, d), lambda i: (_row_group(i, tm), shift_chunk, 0, 0)),
            pl.BlockSpec((None, None, 1, d), lambda i: (_row_group(i, tm), scale_chunk, 0, 0)),
        ]
        args += [mods, mods]
    out_specs = pl.BlockSpec((tm, d), lambda i: (i, 0))
    out_shape = jax.ShapeDtypeStruct((m, d), out_dtype)
    if has_router:
        in_specs.append(pl.BlockSpec((d, ROUTER_LANES), lambda i: (0, 0)))
        args.append(router_w)
        out_specs = [out_specs, pl.BlockSpec((tm, ROUTER_LANES), lambda i: (i, 0))]
        out_shape = [out_shape, jax.ShapeDtypeStruct((m, ROUTER_LANES), jnp.float32)]
    return pl.pallas_call(
        functools.partial(_norm_kernel, has_mod=has_mod, has_router=has_router),
        name="norm_router" if has_router else "norm",
        grid=(m // tm,),
        in_specs=in_specs,
        out_specs=out_specs,
        out_shape=out_shape,
        compiler_params=_params("parallel"),
    )(*args)


def _resid_kernel(x_ref, y_ref, gate_ref, o_ref):
    o_ref[...] = x_ref[...] + gate_ref[...] * y_ref[...].astype(jnp.float32)


def _gated_residual(x, y, mods, gate_chunk, tm=256):
    m, d = x.shape
    return pl.pallas_call(
        _resid_kernel,
        name="gated_residual",
        grid=(m // tm,),
        in_specs=[
            pl.BlockSpec((tm, d), lambda i: (i, 0)),
            pl.BlockSpec((tm, d), lambda i: (i, 0)),
            pl.BlockSpec((None, None, 1, d), lambda i: (_row_group(i, tm), gate_chunk, 0, 0)),
        ],
        out_specs=pl.BlockSpec((tm, d), lambda i: (i, 0)),
        out_shape=jax.ShapeDtypeStruct((m, d), jnp.float32),
        compiler_params=_params("parallel"),
    )(x, y, mods)


def _moe_kernel(be_ref, first_ref, valid_ref, next_ref, x_ref, wg_hbm, wu_hbm, wd_hbm, o_ref,
                stage_g, stage_u, stage_d, wg_bf, wu_bf, wd_bf, sems, *, layer):
    i = pl.program_id(0)

    def weight_copies(e):
        return (pltpu.make_async_copy(wg_hbm.at[layer, e], stage_g, sems.at[0]),
                pltpu.make_async_copy(wu_hbm.at[layer, e], stage_u, sems.at[1]),
                pltpu.make_async_copy(wd_hbm.at[layer, e], stage_d, sems.at[2]))

    @pl.when(i == 0)
    def _():
        for cp in weight_copies(be_ref[0]):
            cp.start()

    @pl.when(first_ref[i] == 1)
    def _():
        for cp in weight_copies(be_ref[i]):
            cp.wait()
        wg_bf[...] = stage_g[...].astype(jnp.bfloat16)
        wu_bf[...] = stage_u[...].astype(jnp.bfloat16)
        wd_bf[...] = stage_d[...].astype(jnp.bfloat16)

        @pl.when(next_ref[i] >= 0)
        def _():
            for cp in weight_copies(next_ref[i]):
                cp.start()

    @pl.when(valid_ref[i] == 1)
    def _():
        x = x_ref[...]
        g = jnp.dot(x, wg_bf[...], preferred_element_type=jnp.float32)
        u = jnp.dot(x, wu_bf[...], preferred_element_type=jnp.float32)
        a = (g * jax.nn.sigmoid(g) * u).astype(jnp.bfloat16)
        o_ref[...] = jnp.dot(a, wd_bf[...], preferred_element_type=jnp.float32).astype(o_ref.dtype)

    @pl.when(valid_ref[i] == 0)
    def _():
        o_ref[...] = jnp.zeros_like(o_ref)


def _moe_ffn(xg, block_expert, block_first, block_valid, block_next, w_gate, w_up, w_down, layer):
    n_rows, d = xg.shape
    n_blocks = n_rows // MOE_BLOCK
    grid_spec = pltpu.PrefetchScalarGridSpec(
        num_scalar_prefetch=4,
        grid=(n_blocks,),
        in_specs=[
            pl.BlockSpec((MOE_BLOCK, d), lambda i, be, bf, bv, bn: (i, 0)),
            _any_spec(), _any_spec(), _any_spec(),
        ],
        out_specs=pl.BlockSpec((MOE_BLOCK, d), lambda i, be, bf, bv, bn: (i, 0)),
        scratch_shapes=[
            pltpu.VMEM((d, MOE_FF), jnp.float32),
            pltpu.VMEM((d, MOE_FF), jnp.float32),
            pltpu.VMEM((MOE_FF, d), jnp.float32),
            pltpu.VMEM((d, MOE_FF), jnp.bfloat16),
            pltpu.VMEM((d, MOE_FF), jnp.bfloat16),
            pltpu.VMEM((MOE_FF, d), jnp.bfloat16),
            pltpu.SemaphoreType.DMA((3,)),
        ],
    )
    return pl.pallas_call(
        functools.partial(_moe_kernel, layer=layer),
        name="moe_ffn",
        grid_spec=grid_spec,
        out_shape=jax.ShapeDtypeStruct((n_rows, d), jnp.bfloat16),
        compiler_params=_params("arbitrary"),
    )(block_expert, block_first, block_valid, block_next, xg, w_gate, w_up, w_down)


def _route(logits):
    t = logits.shape[0]
    g_logits = logits[:, :MOE_GROUPS]
    g_sel = jnp.argmax(g_logits, axis=-1)
    g_prob = jnp.take_along_axis(jax.nn.softmax(g_logits, axis=-1), g_sel[:, None], axis=1)
    e_logits = logits[:, MOE_GROUPS:MOE_GROUPS + N_EXPERTS].reshape(t, MOE_GROUPS, MOE_PER_GROUP)
    e_in_group = jnp.take_along_axis(e_logits, g_sel[:, None, None], axis=1)[:, 0]
    top_val, top_idx = lax.top_k(e_in_group, MOE_TOP_K)
    gates = g_prob * jax.nn.softmax(top_val, axis=-1)
    expert = (g_sel[:, None] * MOE_PER_GROUP + top_idx).astype(jnp.int32)
    return gates, expert


def _moe_layout(expert):
    n_assign = expert.size
    e_flat = expert.reshape(-1)
    onehot = (e_flat[:, None] == jnp.arange(N_EXPERTS, dtype=jnp.int32)[None, :]).astype(jnp.int32)
    csum = jnp.cumsum(onehot, axis=0)
    rank = jnp.take_along_axis(csum, e_flat[:, None], axis=1)[:, 0] - 1
    counts = csum[-1]
    padded = (counts + MOE_BLOCK - 1) // MOE_BLOCK * MOE_BLOCK
    p_ends = jnp.cumsum(padded)
    p_starts = p_ends - padded
    pos = (p_starts[e_flat] + rank).astype(jnp.int32)
    n_blocks = -(-n_assign // MOE_BLOCK) + N_EXPERTS
    n_rows = n_blocks * MOE_BLOCK
    token = jnp.repeat(jnp.arange(n_assign // MOE_TOP_K, dtype=jnp.int32), MOE_TOP_K)
    row_token = jnp.zeros((n_rows,), jnp.int32).at[pos].set(token)
    block_start = jnp.arange(n_blocks, dtype=jnp.int32) * MOE_BLOCK
    block_expert = jnp.minimum(jnp.searchsorted(p_ends, block_start, side='right'),
                               N_EXPERTS - 1).astype(jnp.int32)
    valid = block_start < p_ends[-1]
    prev = jnp.concatenate([jnp.full((1,), -1, jnp.int32), block_expert[:-1]])
    block_first = (valid & (block_expert != prev)).astype(jnp.int32)
    after = jnp.minimum(p_ends[block_expert] // MOE_BLOCK, n_blocks - 1)
    has_next = p_ends[block_expert] < p_ends[-1]
    block_next = jnp.where(has_next, block_expert[after], -1).astype(jnp.int32)
    return pos.reshape(expert.shape), row_token, block_expert, block_first, valid.astype(jnp.int32), block_next


def _hier_moe(h, logits, w_gate, w_up, w_down, layer):
    gates, expert = _route(logits)
    pos, row_token, block_expert, block_first, block_valid, block_next = _moe_layout(expert)
    xg = jnp.take(h, row_token, axis=0)
    yb = _moe_ffn(xg, block_expert, block_first, block_valid, block_next, w_gate, w_up, w_down, layer)
    return (jnp.take(yb, pos[:, 0], axis=0).astype(jnp.float32) * gates[:, 0:1]
            + jnp.take(yb, pos[:, 1], axis=0).astype(jnp.float32) * gates[:, 1:2])


def _log_sigmoid(x):
    return jnp.minimum(x, 0.0) - jnp.log(1.0 + jnp.exp(-jnp.abs(x)))


def _softplus(x):
    return jnp.maximum(x, 0.0) + jnp.log(1.0 + jnp.exp(-jnp.abs(x)))


def _mlstm_kernel(*refs, seq, has_init, emit_state, n_alias, state_slot):
    f32, bf16 = jnp.float32, jnp.bfloat16
    it = iter(refs)
    q_ref, k_ref, v_ref, o_ref, gc_ref, gr_ref, nw_ref = (next(it) for _ in range(7))
    if has_init:
        c0_ref, n0_ref, m0_ref = next(it), next(it), next(it)
    for _ in range(n_alias):
        next(it)
    y_ref = next(it)
    if emit_state:
        cout_ref, nout_ref, mout_ref = next(it), next(it), next(it)
    hacc, c_sc, n_sc, m_sc = next(it), next(it), next(it), next(it)

    L = MLSTM_CHUNK
    nc = seq // L
    use_inter = has_init or nc > 1
    scale = MLSTM_DQK ** -0.5
    row = lax.broadcasted_iota(jnp.int32, (L, L), 0)
    col = lax.broadcasted_iota(jnp.int32, (L, L), 1)

    for d in range(2):
        tri = (col <= row) if d == 0 else (col >= row)
        tri_t = (row <= col) if d == 0 else (row >= col)
        if has_init:
            c_sc[...] = c0_ref[d]
            n_sc[...] = n0_ref[d]
            m_sc[...] = m0_ref[d]
        else:
            c_sc[...] = jnp.zeros_like(c_sc)
            n_sc[...] = jnp.zeros_like(n_sc)
            m_sc[...] = jnp.zeros_like(m_sc)

        def chunk(ci, carry, d=d, tri=tri, tri_t=tri_t):
            c = ci if d == 0 else nc - 1 - ci
            rows = _chunk_rows(c, L)
            q, k, v = q_ref[rows, :], k_ref[rows, :], v_ref[rows, :]
            gcol, grow = gc_ref[c], gr_ref[c]
            i_col = gcol[:, 2 * d:2 * d + 1]
            f_col = _log_sigmoid(gcol[:, 2 * d + 1:2 * d + 2])
            i_row = grow[2 * d:2 * d + 1, :]
            f_row = _log_sigmoid(grow[2 * d + 1:2 * d + 2, :])
            b_col = jnp.sum(jnp.where(tri, f_row, 0.0), axis=1, keepdims=True)
            b_row = jnp.sum(jnp.where(tri_t, f_col, 0.0), axis=0, keepdims=True)
            b_end = jnp.sum(f_row, axis=1, keepdims=True)
            m_prev = m_sc[...]
            log_d = jnp.where(tri, b_col - b_row + i_row, -jnp.inf)
            log_inter = b_col + m_prev
            m_t = jnp.maximum(log_inter, jnp.max(log_d, axis=1, keepdims=True))
            w_intra = jnp.exp(log_d - m_t)
            qk = lax.dot_general(q, k, (((1,), (1,)), ((), ())), preferred_element_type=f32) * (scale * w_intra)
            num = jnp.dot(qk.astype(bf16), v, preferred_element_type=f32)
            den = jnp.sum(qk, axis=1, keepdims=True)
            if use_inter:
                w_inter = jnp.exp(log_inter - m_t) * scale
                num = num + w_inter * jnp.dot(q, c_sc[...].astype(bf16), preferred_element_type=f32)
                den = den + w_inter * jnp.sum(q.astype(f32) * n_sc[...], axis=1, keepdims=True)
            h = num / jnp.maximum(jnp.abs(den), jnp.exp(-m_t))
            if d == 0:
                hacc[rows, :] = h
            else:
                hacc[rows, :] = hacc[rows, :] + h
            if emit_state or nc > 1:
                log_w = b_end - b_col + i_col
                m_new = jnp.maximum(b_end + m_prev, jnp.max(log_w, axis=0, keepdims=True))
                kw = k.astype(f32) * jnp.exp(log_w - m_new)
                decay = jnp.exp(b_end + m_prev - m_new)
                c_sc[...] = decay * c_sc[...] + jnp.dot(kw.T.astype(bf16), v, preferred_element_type=f32)
                n_sc[...] = decay * n_sc[...] + jnp.sum(kw, axis=0, keepdims=True)
                m_sc[...] = m_new
            return carry

        if nc == 1:
            chunk(0, 0)
        else:
            lax.fori_loop(0, nc, chunk, 0)
        if emit_state and state_slot is None:
            cout_ref[d] = c_sc[...]
            nout_ref[d] = n_sc[...]
            mout_ref[d] = m_sc[...]
        elif emit_state:
            for slot in range(cout_ref.shape[0]):
                if slot == state_slot:
                    cout_ref[slot, d] = c_sc[...]
                    nout_ref[slot, d] = n_sc[...]
                    mout_ref[slot, d] = m_sc[...]
                else:
                    cout_ref[slot, d] = jnp.zeros_like(c_sc)
                    nout_ref[slot, d] = jnp.zeros_like(n_sc)
                    mout_ref[slot, d] = jnp.zeros_like(m_sc)

    hs = hacc[...]
    hn = hs * lax.rsqrt(jnp.mean(hs * hs, axis=-1, keepdims=True) + EPS) * nw_ref[...]
    y_ref[...] = (hn * jax.nn.sigmoid(o_ref[...].astype(f32))).astype(y_ref.dtype)


def _mlstm_seq(proj, gates, norm_w, layer, *, row0, nb, seq, init=None, state_prev=None,
               emit_state=False, n_layers=1):
    f32 = jnp.float32
    H, L = MLSTM_HEADS, MLSTM_CHUNK
    nc = seq // L
    rb0 = row0 // seq
    g = gates[row0:row0 + nb * seq].reshape(nb, nc, L, 2, 2, H)
    g_col = g.transpose(0, 5, 1, 2, 3, 4).reshape(nb, H, nc, L, 4)
    g_row = g.transpose(0, 5, 1, 3, 4, 2).reshape(nb, H, nc, 4, L)
    qb, vb = MLSTM_DQK, MLSTM_DV
    in_specs = [
        pl.BlockSpec((seq, qb), lambda b, h: (rb0 + b, h)),
        pl.BlockSpec((seq, qb), lambda b, h: (rb0 + b, H + h)),
        pl.BlockSpec((seq, vb), lambda b, h: (rb0 + b, H + h)),
        pl.BlockSpec((seq, vb), lambda b, h: (rb0 + b, 2 * H + h)),
        pl.BlockSpec((None, None, nc, L, 4), lambda b, h: (b, h, 0, 0, 0)),
        pl.BlockSpec((None, None, nc, 4, L), lambda b, h: (b, h, 0, 0, 0)),
        pl.BlockSpec((None, None, 1, vb), lambda b, h: (layer, h, 0, 0)),
    ]
    args = [proj, proj, proj, proj, g_col, g_row, norm_w.reshape(-1, H, 1, vb)]
    has_init = init is not None
    if has_init:
        c0, n0, m0 = init
        in_specs += [
            pl.BlockSpec((None, None, 2, None, qb, vb), lambda b, h: (b, layer, 0, h, 0, 0)),
            pl.BlockSpec((None, None, 2, None, 1, qb), lambda b, h: (b, layer, 0, h, 0, 0)),
            pl.BlockSpec((None, None, 2, None, 1, 1), lambda b, h: (b, layer, 0, h, 0, 0)),
        ]
        args += [c0, n0.reshape(*n0.shape[:4], 1, qb), m0.reshape(*m0.shape[:4], 1, 1)]
    out_specs = [pl.BlockSpec((seq, vb), lambda b, h: (b, h))]
    out_shape = [jax.ShapeDtypeStruct((nb * seq, H * vb), jnp.bfloat16)]
    state_slot = None
    if emit_state and state_prev is None:
        state_slot = layer
        out_specs += [
            pl.BlockSpec((None, n_layers, 2, None, qb, vb), lambda b, h: (b, 0, 0, h, 0, 0)),
            pl.BlockSpec((None, n_layers, 2, None, 1, qb), lambda b, h: (b, 0, 0, h, 0, 0)),
            pl.BlockSpec((None, n_layers, 2, None, 1, 1), lambda b, h: (b, 0, 0, h, 0, 0)),
        ]
    elif emit_state:
        out_specs += [
            pl.BlockSpec((None, None, 2, None, qb, vb), lambda b, h: (b, layer, 0, h, 0, 0)),
            pl.BlockSpec((None, None, 2, None, 1, qb), lambda b, h: (b, layer, 0, h, 0, 0)),
            pl.BlockSpec((None, None, 2, None, 1, 1), lambda b, h: (b, layer, 0, h, 0, 0)),
        ]
    if emit_state:
        out_shape += [
            jax.ShapeDtypeStruct((nb, n_layers, 2, H, qb, vb), f32),
            jax.ShapeDtypeStruct((nb, n_layers, 2, H, 1, qb), f32),
            jax.ShapeDtypeStruct((nb, n_layers, 2, H, 1, 1), f32),
        ]
    aliases = {}
    n_alias = 0
    if state_prev is not None:
        for k_out, a in enumerate(state_prev):
            aliases[len(args)] = 1 + k_out
            in_specs.append(_any_spec())
            args.append(a)
            n_alias += 1
    return pl.pallas_call(
        functools.partial(_mlstm_kernel, seq=seq, has_init=has_init, emit_state=emit_state, n_alias=n_alias,
                          state_slot=state_slot),
        name=f"mlstm_s{seq}",
        grid=(nb, H),
        in_specs=in_specs,
        out_specs=out_specs,
        out_shape=out_shape,
        input_output_aliases=aliases,
        scratch_shapes=[
            pltpu.VMEM((seq, vb), f32),
            pltpu.VMEM((qb, vb), f32),
            pltpu.VMEM((1, qb), f32),
            pltpu.VMEM((1, 1), f32),
        ],
        compiler_params=_params("arbitrary", "arbitrary"),
    )(*args)


def _conv_silu(x, w_ref, b_ref):
    s = x.shape[0]
    t = lax.broadcasted_iota(jnp.int32, x.shape, 0)
    y = (w_ref[0:1, :] * jnp.where(t >= 2, pltpu.roll(x, 2, 0), 0.0)
         + w_ref[1:2, :] * jnp.where(t >= 1, pltpu.roll(x, 1, 0), 0.0)
         + w_ref[2:3, :] * x
         + w_ref[3:4, :] * jnp.where(t < s - 1, pltpu.roll(x, s - 1, 0), 0.0)
         + b_ref[...])
    return y * jax.nn.sigmoid(y)


def _ssd_kernel(*refs, seq, has_init, emit_state):
    f32, bf16 = jnp.float32, jnp.bfloat16
    it = iter(refs)
    (x_ref, b_ref, c_ref, z_ref, dtc_ref, dtr_ref, dtbc_ref, dtbr_ref, alc_ref, alr_ref,
     cwx_ref, cbx_ref, cwb_ref, cbb_ref, cwc_ref, cbc_ref, dsk_ref, nw_ref) = (next(it) for _ in range(18))
    if has_init:
        h0_ref = next(it)
    y_ref = next(it)
    if emit_state:
        hout_ref = next(it)
    xc_sc, bc_sc, cc_sc, yacc, hst = (next(it) for _ in range(5))

    L = SSD_CHUNK
    nc = seq // L
    E, P = SSD_GROUP_HEADS, SSD_HEAD_DIM
    n_pair = E // 2
    W = 2 * P

    for p in range(n_pair):
        cols = slice(p * W, (p + 1) * W)
        xc_sc[:, cols] = _conv_silu(x_ref[:, cols].astype(f32), cwx_ref.at[:, cols], cbx_ref.at[:, cols])
    bc_sc[...] = _conv_silu(b_ref[...].astype(f32), cwb_ref, cbb_ref).astype(bf16)
    cc_sc[...] = _conv_silu(c_ref[...].astype(f32), cwc_ref, cbc_ref).astype(bf16)

    row = lax.broadcasted_iota(jnp.int32, (L, L), 0)
    col = lax.broadcasted_iota(jnp.int32, (L, L), 1)
    lo = lax.broadcasted_iota(jnp.int32, (L, W), 1) < P
    lo_rows = lax.broadcasted_iota(jnp.int32, (W, SSD_STATE), 0) < P
    a_col = -jnp.exp(alc_ref[...])
    a_row = -jnp.exp(alr_ref[...])

    for d in range(2):
        tri = (col <= row) if d == 0 else (col >= row)
        tri_f = tri.astype(f32)
        tri_tf = ((row <= col) if d == 0 else (row >= col)).astype(f32)
        if has_init:
            hst[...] = h0_ref[d]
        else:
            hst[...] = jnp.zeros_like(hst)

        def chunk(ci, carry, d=d, tri=tri, tri_f=tri_f, tri_tf=tri_tf):
            c = ci if d == 0 else nc - 1 - ci
            rows = _chunk_rows(c, L)
            dt_c = _softplus(dtc_ref[c] + dtbc_ref[...])
            dt_r = _softplus(dtr_ref[c] + dtbr_ref[...])
            cs_c = jnp.dot(tri_f, dt_c * a_col, preferred_element_type=f32, precision=lax.Precision.HIGHEST)
            cs_r = jnp.dot(dt_r * a_row, tri_tf, preferred_element_type=f32, precision=lax.Precision.HIGHEST)
            cs_e = cs_c[L - 1:L, :] if d == 0 else cs_c[0:1, :]
            bm, cm = bc_sc[rows, :], cc_sc[rows, :]
            cb = lax.dot_general(cm, bm, (((1,), (1,)), ((), ())), preferred_element_type=f32)
            for p in range(n_pair):
                cols = slice(p * W, (p + 1) * W)
                ia, ib = d * E + 2 * p, d * E + 2 * p + 1
                xdt = xc_sc[rows, cols] * jnp.where(lo, dt_c[:, ia:ia + 1], dt_c[:, ib:ib + 1])
                xdt_bf = xdt.astype(bf16)
                cs_ta, cs_tb = cs_c[:, ia:ia + 1], cs_c[:, ib:ib + 1]
                wa = (cb * jnp.exp(jnp.where(tri, cs_ta - cs_r[ia:ia + 1, :], -jnp.inf))).astype(bf16)
                wb = (cb * jnp.exp(jnp.where(tri, cs_tb - cs_r[ib:ib + 1, :], -jnp.inf))).astype(bf16)
                y = jnp.where(lo, jnp.dot(wa, xdt_bf, preferred_element_type=f32),
                              jnp.dot(wb, xdt_bf, preferred_element_type=f32))
                hrows = slice(p * W, (p + 1) * W)
                hp = hst[hrows, :]
                if has_init or nc > 1:
                    y_inter = lax.dot_general(cm, hp.astype(bf16), (((1,), (1,)), ((), ())),
                                              preferred_element_type=f32)
                    y = y + jnp.where(lo, jnp.exp(cs_ta), jnp.exp(cs_tb)) * y_inter
                if d == 0:
                    yacc[rows, cols] = y
                else:
                    yacc[rows, cols] = yacc[rows, cols] + y
                if emit_state or nc > 1:
                    ea, eb = cs_e[:, ia:ia + 1], cs_e[:, ib:ib + 1]
                    xw = xdt * jnp.where(lo, jnp.exp(ea - cs_ta), jnp.exp(eb - cs_tb))
                    contrib = jnp.dot(xw.T.astype(bf16), bm, preferred_element_type=f32)
                    hst[hrows, :] = jnp.where(lo_rows, jnp.exp(ea), jnp.exp(eb)) * hp + contrib
            return carry

        if nc == 1:
            chunk(0, 0)
        else:
            lax.fori_loop(0, nc, chunk, 0)
        if emit_state:
            hout_ref[d] = hst[...]

    y = yacc[...] + dsk_ref[...] * xc_sc[...]
    z = z_ref[...].astype(f32)
    y = y * (z * jax.nn.sigmoid(z))
    y = y * lax.rsqrt(jnp.mean(y * y, axis=-1, keepdims=True) + EPS) * nw_ref[...]
    y_ref[...] = y.astype(y_ref.dtype)


def _ssd_seq(zx, dt_raw, conv_w, conv_b, dt_bias, a_log, d_skip, norm_w, layer, *, row0, nb, seq,
             init=None, emit_state=False):
    f32 = jnp.float32
    G, E, L, N = SSD_GROUPS, SSD_GROUP_HEADS, SSD_CHUNK, SSD_STATE
    gc = SSD_GROUP_COLS
    nc = seq // L
    rb0 = row0 // seq
    dt = dt_raw[row0:row0 + nb * seq].reshape(nb, nc, L, 2, G, E)
    dt_col = dt.transpose(0, 4, 1, 2, 3, 5).reshape(nb, G, nc, L, 2 * E)
    dt_row = dt.transpose(0, 4, 1, 3, 5, 2).reshape(nb, G, nc, 2 * E, L)

    def per_group(a):
        a = a.astype(f32).reshape(2, G, E).transpose(1, 0, 2).reshape(G, 2 * E)
        return a[:, None, :], a[:, :, None]

    dtb_c, dtb_r = per_group(dt_bias)
    al_c, al_r = per_group(a_log)
    x_cb = SSD_INNER // gc
    bc0 = 2 * SSD_INNER // N
    cw = conv_w.astype(f32)
    cbias = conv_b.astype(f32).reshape(1, -1)
    dsk = jnp.repeat(d_skip.astype(f32), SSD_HEAD_DIM).reshape(1, SSD_INNER)
    in_specs = [
        pl.BlockSpec((seq, gc), lambda b, g: (rb0 + b, x_cb + g)),
        pl.BlockSpec((seq, N), lambda b, g: (rb0 + b, bc0 + g)),
        pl.BlockSpec((seq, N), lambda b, g: (rb0 + b, bc0 + G + g)),
        pl.BlockSpec((seq, gc), lambda b, g: (rb0 + b, g)),
        pl.BlockSpec((None, None, nc, L, 2 * E), lambda b, g: (b, g, 0, 0, 0)),
        pl.BlockSpec((None, None, nc, 2 * E, L), lambda b, g: (b, g, 0, 0, 0)),
        pl.BlockSpec((None, 1, 2 * E), lambda b, g: (g, 0, 0)),
        pl.BlockSpec((None, 2 * E, 1), lambda b, g: (g, 0, 0)),
        pl.BlockSpec((None, 1, 2 * E), lambda b, g: (g, 0, 0)),
        pl.BlockSpec((None, 2 * E, 1), lambda b, g: (g, 0, 0)),
        pl.BlockSpec((SSD_CONV, gc), lambda b, g: (0, g)),
        pl.BlockSpec((1, gc), lambda b, g: (0, g)),
        pl.BlockSpec((SSD_CONV, N), lambda b, g: (0, SSD_INNER // N + g)),
        pl.BlockSpec((1, N), lambda b, g: (0, SSD_INNER // N + g)),
        pl.BlockSpec((SSD_CONV, N), lambda b, g: (0, SSD_INNER // N + G + g)),
        pl.BlockSpec((1, N), lambda b, g: (0, SSD_INNER // N + G + g)),
        pl.BlockSpec((1, gc), lambda b, g: (0, g)),
        pl.BlockSpec((1, gc), lambda b, g: (0, g)),
    ]
    args = [zx, zx, zx, zx, dt_col, dt_row, dtb_c, dtb_r, al_c, al_r,
            cw, cbias, cw, cbias, cw, cbias, dsk, norm_w.astype(f32).reshape(1, SSD_INNER)]
    has_init = init is not None
    state_block = pl.BlockSpec((None, None, 2, None, E * SSD_HEAD_DIM, N), lambda b, g: (b, layer, 0, g, 0, 0))
    if has_init:
        in_specs.append(state_block)
        args.append(init.reshape(*init.shape[:3], G, E * SSD_HEAD_DIM, N))
    out_specs = [pl.BlockSpec((seq, gc), lambda b, g: (b, g))]
    out_shape = [jax.ShapeDtypeStruct((nb * seq, SSD_INNER), jnp.bfloat16)]
    if emit_state:
        out_specs.append(state_block)
        out_shape.append(jax.ShapeDtypeStruct((nb, 1, 2, G, E * SSD_HEAD_DIM, N), f32))
    return pl.pallas_call(
        functools.partial(_ssd_kernel, seq=seq, has_init=has_init, emit_state=emit_state),
        name=f"ssd_s{seq}",
        grid=(nb, G),
        in_specs=in_specs,
        out_specs=out_specs,
        out_shape=out_shape,
        scratch_shapes=[
            pltpu.VMEM((seq, gc), f32),
            pltpu.VMEM((seq, N), jnp.bfloat16),
            pltpu.VMEM((seq, N), jnp.bfloat16),
            pltpu.VMEM((seq, gc), f32),
            pltpu.VMEM((E * SSD_HEAD_DIM, N), f32),
        ],
        compiler_params=_params("arbitrary", "arbitrary"),
    )(*args)


def _attn_kernel(*refs, heads, rope):
    f32, bf16 = jnp.float32, jnp.bfloat16
    if rope:
        qn_ref, qp_ref, kv_ref, kpe_ref, cos_ref, sin_ref, o_ref = refs
    else:
        qn_ref, qp_ref, kv_ref, kpe_ref, o_ref = refs
    scale = (MLA_NOPE + MLA_ROPE) ** -0.5
    kpe2 = kpe_ref[...]
    lq = qn_ref.shape[0]
    lane = lax.broadcasted_iota(jnp.int32, (lq, V7X_LANES), 1)
    nt = (((1,), (1,)), ((), ()))
    for pr in range(heads // 2):
        qp = qp_ref[:, pr * V7X_LANES:(pr + 1) * V7X_LANES]
        if rope:
            x = qp.astype(f32)
            other = jnp.where(lane % (2 * ROPE_FREQS) < ROPE_FREQS,
                              pltpu.roll(x, V7X_LANES - ROPE_FREQS, 1), pltpu.roll(x, ROPE_FREQS, 1))
            qp = (x * cos_ref[...] + other * sin_ref[...]).astype(bf16)
        for a in range(2):
            h = 2 * pr + a
            qpa = jnp.where((lane < MLA_ROPE) if a == 0 else (lane >= MLA_ROPE), qp, jnp.zeros_like(qp))
            kn = kv_ref[:, h * 2 * MLA_NOPE:h * 2 * MLA_NOPE + MLA_NOPE]
            vv = kv_ref[:, h * 2 * MLA_NOPE + MLA_NOPE:(h + 1) * 2 * MLA_NOPE]
            s = (lax.dot_general(qn_ref[:, h * MLA_NOPE:(h + 1) * MLA_NOPE], kn, nt, preferred_element_type=f32)
                 + lax.dot_general(qpa, kpe2, nt, preferred_element_type=f32)) * scale
            e = jnp.exp(s - jnp.max(s, axis=-1, keepdims=True))
            p = e / jnp.sum(e, axis=-1, keepdims=True)
            o_ref[:, h * MLA_V:(h + 1) * MLA_V] = jnp.dot(p.astype(bf16), vv,
                                                          preferred_element_type=f32).astype(o_ref.dtype)


def _attention(qn, qp, kv, kpe2, *, row0, nb, seq, key0, n_keys, q_tile, heads, tables=None):
    nq = seq // q_tile
    qb0 = row0 // q_tile
    kb0 = key0 // n_keys
    assert row0 % q_tile == 0 and key0 % n_keys == 0
    hb = MLA_HEADS // heads
    in_specs = [
        pl.BlockSpec((q_tile, heads * MLA_NOPE), lambda b, t, h: (qb0 + b * nq + t, h)),
        pl.BlockSpec((q_tile, heads * MLA_ROPE), lambda b, t, h: (qb0 + b * nq + t, h)),
        pl.BlockSpec((n_keys, heads * 2 * MLA_NOPE), lambda b, t, h: (kb0 + b, h)),
        pl.BlockSpec((n_keys, V7X_LANES), lambda b, t, h: (kb0 + b, 0)),
    ]
    args = [qn, qp, kv, kpe2]
    rope = tables is not None
    if rope:
        in_specs += [pl.BlockSpec((q_tile, V7X_LANES), lambda b, t, h: (t, 0))] * 2
        args += list(tables)
    return pl.pallas_call(
        functools.partial(_attn_kernel, heads=heads, rope=rope),
        name=f"attn_s{seq}",
        grid=(nb, nq, hb),
        in_specs=in_specs,
        out_specs=pl.BlockSpec((q_tile, heads * MLA_V), lambda b, t, h: (b * nq + t, h)),
        out_shape=jax.ShapeDtypeStruct((nb * seq, MLA_HEADS * MLA_V), jnp.bfloat16),
        compiler_params=_params("arbitrary", "arbitrary", "arbitrary"),
    )(*args)


def _rope_tables(n_tok):
    rows = n_tok // GRID_W
    row = jnp.repeat(jnp.arange(rows, dtype=jnp.float32), GRID_W)
    col = jnp.tile(jnp.arange(GRID_W, dtype=jnp.float32), rows)
    inv_freq = ROPE_THETA ** (-jnp.arange(ROPE_FREQS, dtype=jnp.float32) / ROPE_FREQS)
    ang = jnp.stack([row, col], axis=-1)[:, :, None] * inv_freq
    cos, sin = jnp.cos(ang), jnp.sin(ang)
    cos64 = jnp.concatenate([cos[:, 0], cos[:, 0], cos[:, 1], cos[:, 1]], axis=-1)
    sin64 = jnp.concatenate([-sin[:, 0], sin[:, 0], -sin[:, 1], sin[:, 1]], axis=-1)
    return jnp.tile(cos64, (1, 2)), jnp.tile(sin64, (1, 2))


def _rope_rows(x, cos64, sin64):
    xr = x.reshape(x.shape[0], 2, 2, ROPE_FREQS)
    other = jnp.stack([xr[:, :, 1], xr[:, :, 0]], axis=2).reshape(x.shape)
    return x * cos64 + other * sin64


def kernel(x_prompt, x_sample, state_mlstm_C, state_mlstm_n, state_mlstm_m, state_ssm, cache_mla_ckv, cache_mla_kpe, c, c_ctx, ada_w, ada_b, norm_mix, norm_ffn, mlstm_w_in, mlstm_b_gates, mlstm_norm, mlstm_w_out, ssd_w_in, ssd_conv_w, ssd_conv_b, ssd_dt_bias, ssd_A_log, ssd_D, ssd_norm, ssd_w_out, mla_w_in, mla_q_norm, mla_kv_norm, mla_w_qb, mla_w_kvb, mla_w_out, router_group_w, router_expert_w, moe_w_gate, moe_w_up, moe_w_down, final_norm):
    f32, bf16 = jnp.float32, jnp.bfloat16
    D = D_MODEL
    X = jnp.concatenate([x_prompt.reshape(N_PROMPT, D), x_sample.reshape(N_SAMPLE, D)], axis=0)

    cond = jnp.concatenate([c_ctx[None, :], c], axis=0)
    cond = jnp.pad(jax.nn.silu(cond), ((0, 16 - cond.shape[0]), (0, 0))).astype(bf16)

    n_mlstm = mlstm_w_in.shape[0]
    mlstm_state = None
    new_ssm, new_ckv, new_kpe = [], [], []
    for l in range(DEPTH):
        kind, j = l % N_MIXERS, l // N_MIXERS
        ada = _matmul(cond, ada_w, layer=l, tm=16, tn=512, out_dtype=f32)
        mods = (ada[:1 + DEC_BATCH] + ada_b[l]).reshape(1 + DEC_BATCH, N_MOD, 1, D)
        h = _norm(X, norm_mix[l], mods=mods, shift_chunk=0, scale_chunk=1)

        if kind == 0:
            proj = _matmul(h, mlstm_w_in, layer=j, n=12288)
            n_gate = 4 * MLSTM_HEADS
            gates = _matmul(h, mlstm_w_in[j, :, 12288:], tn=n_gate, out_dtype=f32) + mlstm_b_gates[j]
            outs = _mlstm_seq(proj, gates, mlstm_norm, j, row0=0, nb=BATCH, seq=SEQ, emit_state=True,
                              state_prev=mlstm_state, n_layers=n_mlstm)
            y_p, mlstm_state = outs[0], tuple(outs[1:])
            y_s = _mlstm_seq(proj, gates, mlstm_norm, j, row0=N_PROMPT, nb=DEC_BATCH, seq=DEC_SEQ,
                             init=(state_mlstm_C, state_mlstm_n, state_mlstm_m))[0]
            X = _matmul((y_p, y_s), mlstm_w_out, layer=j, tm=512, out_dtype=f32, res=X, mods=mods, gate_chunk=2)
        elif kind == 1:
            zx = _matmul(h, ssd_w_in, layer=j, n=18432)
            dt_raw = _matmul(h, ssd_w_in, layer=j, col0=18432, n=256, tn=256, out_dtype=f32)
            w = (ssd_conv_w[j], ssd_conv_b[j], ssd_dt_bias[j], ssd_A_log[j], ssd_D[j], ssd_norm[j])
            y_p, s_h = _ssd_seq(zx, dt_raw, *w, 0, row0=0, nb=BATCH, seq=SEQ, emit_state=True)
            y_s = _ssd_seq(zx, dt_raw, *w, j, row0=N_PROMPT, nb=DEC_BATCH, seq=DEC_SEQ, init=state_ssm)[0]
            new_ssm.append(s_h.reshape(BATCH, 2, SSD_HEADS, SSD_HEAD_DIM, SSD_STATE))
            X = _matmul(jnp.concatenate([y_p, y_s], axis=0), ssd_w_out, layer=j, tm=512, tn=512, out_dtype=f32, res=X, mods=mods, gate_chunk=2,
                        single_buffer_w=True)
        else:
            lat = _matmul(h, mla_w_in, layer=j, n=2048, out_dtype=f32)
            kpe = _matmul(h, mla_w_in[j, :, 2048:], tn=MLA_ROPE, out_dtype=f32)
            hq = _norm(lat[:, :MLA_Q_RANK], mla_q_norm[j])
            ckv = _norm(lat[:, MLA_Q_RANK:], mla_kv_norm[j], out_dtype=f32)
            wq = mla_w_qb[j].reshape(MLA_Q_RANK, MLA_HEADS, MLA_NOPE + MLA_ROPE)
            qn = _matmul(hq, wq[:, :, :MLA_NOPE].reshape(MLA_Q_RANK, -1), tn=512)
            qp = _matmul(hq, wq[:, :, MLA_NOPE:].reshape(MLA_Q_RANK, -1), tn=512)
            n_keys = PAST_LEN + DEC_SEQ
            n_skeys = DEC_BATCH * n_keys
            cos2, sin2 = _rope_tables(DEC_SEQ)
            ckv_s = jnp.concatenate([cache_mla_ckv[:, j], ckv[N_PROMPT:].reshape(DEC_BATCH, DEC_SEQ, -1)], axis=1)
            ckv_all = jnp.concatenate([ckv_s.reshape(-1, MLA_KV_RANK), ckv[:N_PROMPT]], axis=0).astype(bf16)
            kv = _matmul(ckv_all, mla_w_kvb, layer=j, tn=512)
            kpe_lat = _rope_rows(kpe[N_PROMPT:], jnp.tile(cos2[:, :MLA_ROPE], (DEC_BATCH, 1)),
                                 jnp.tile(sin2[:, :MLA_ROPE], (DEC_BATCH, 1)))
            kpe_s = jnp.concatenate([cache_mla_kpe[:, j], kpe_lat.reshape(DEC_BATCH, DEC_SEQ, -1)], axis=1)
            kpe_all = jnp.concatenate([kpe_s.reshape(-1, MLA_ROPE), kpe[:N_PROMPT]], axis=0).astype(bf16)
            kpe2 = jnp.concatenate([kpe_all, kpe_all], axis=1)
            o_p = _attention(qn, qp, kv, kpe2, row0=0, nb=BATCH, seq=SEQ, key0=n_skeys, n_keys=SEQ, q_tile=SEQ,
                             heads=8)
            o_s = _attention(qn, qp, kv, kpe2, row0=N_PROMPT, nb=DEC_BATCH, seq=DEC_SEQ, key0=0,
                             n_keys=n_keys, q_tile=ATTN_Q_TILE, heads=2, tables=(cos2, sin2))
            new_ckv.append(ckv[:N_PROMPT].reshape(BATCH, SEQ, MLA_KV_RANK))
            new_kpe.append(kpe[:N_PROMPT].reshape(BATCH, SEQ, MLA_ROPE))
            X = _matmul(jnp.concatenate([o_p, o_s], axis=0), mla_w_out, layer=j, tm=512, tn=512, out_dtype=f32, res=X, mods=mods, gate_chunk=2,
                        single_buffer_w=True)

        router_w = jnp.pad(jnp.concatenate([router_group_w[l], router_expert_w[l]], axis=1),
                           ((0, 0), (0, ROUTER_LANES - MOE_GROUPS - N_EXPERTS)))
        hf, logits = _norm(X, norm_ffn[l], mods=mods, shift_chunk=3, scale_chunk=4, router_w=router_w)
        y = _hier_moe(hf, logits, moe_w_gate, moe_w_up, moe_w_down, l)
        X = _gated_residual(X, y, mods, 5)

    Y = _norm(X, final_norm, out_dtype=f32)
    y_prompt = Y[:N_PROMPT].reshape(BATCH, SEQ, D)
    y_sample = Y[N_PROMPT:].reshape(DEC_BATCH, DEC_SEQ, D)
    new_c, new_n, new_m = mlstm_state
    return (y_prompt, y_sample, new_c,
            new_n.reshape(BATCH, n_mlstm, 2, MLSTM_HEADS, MLSTM_DQK),
            new_m.reshape(BATCH, n_mlstm, 2, MLSTM_HEADS),
            jnp.stack(new_ssm, axis=1), jnp.stack(new_ckv, axis=1), jnp.stack(new_kpe, axis=1))
```

```python
import functools

import jax
import jax.numpy as jnp
from jax import lax
from jax.experimental import pallas as pl
from jax.experimental.pallas import tpu as pltpu

D_MODEL = 4096
BATCH = 32
SEQ = 256
DEPTH = 4
DEC_BATCH = 2
DEC_SEQ = 1024
PAST_LEN = 512
GRID_W = 64
N_MIXERS = 3
EPS = 1e-6

MLSTM_HEADS = 8
MLSTM_DQK = 256
MLSTM_DV = 512

SSD_INNER = 8192
SSD_HEAD_DIM = 64
SSD_HEADS = 128
SSD_GROUPS = 8
SSD_STATE = 128
SSD_CONV = 4
SSD_GROUP_HEADS = SSD_HEADS // SSD_GROUPS
SSD_GROUP_COLS = SSD_INNER // SSD_GROUPS

MLA_HEADS = 64
MLA_Q_RANK = 1536
MLA_KV_RANK = 512
MLA_NOPE = 128
MLA_ROPE = 64
MLA_V = 128
ROPE_THETA = 10000.0
ROPE_FREQS = 16

MOE_GROUPS = 8
MOE_PER_GROUP = 8
N_EXPERTS = 64
MOE_TOP_K = 2
MOE_FF = 512

N_PROMPT = BATCH * SEQ
N_SAMPLE = DEC_BATCH * DEC_SEQ
N_TOK = N_PROMPT + N_SAMPLE
N_MOD = 6

V7X_VMEM_LIMIT = 56 * 1024 * 1024
V7X_LANES = 128
MOE_BLOCK = 256
ROUTER_LANES = V7X_LANES
MLSTM_CHUNK = 256
SSD_CHUNK = 128
ATTN_Q_TILE = 512


def _params(*sem):
    return pltpu.CompilerParams(dimension_semantics=sem, vmem_limit_bytes=V7X_VMEM_LIMIT)


def _row_group(tile, rows_per_tile):
    first_sample = N_PROMPT // rows_per_tile
    per_batch = DEC_SEQ // rows_per_tile
    return jnp.where(tile < first_sample, 0, 1 + (tile - first_sample) // per_batch)


def _any_spec():
    return pl.BlockSpec(memory_space=pl.ANY)


def _chunk_rows(c, length):
    if isinstance(c, int):
        return pl.ds(c * length, length)
    return pl.ds(pl.multiple_of(c * length, length), length)


def _mm_kernel(*refs, has_res, split_tile):
    n_x = 1 if split_tile is None else 2
    x_refs, refs = refs[:n_x], refs[n_x:]
    if has_res:
        w_ref, res_ref, gate_ref, o_ref, wbf_ref = refs
    else:
        w_ref, o_ref, wbf_ref = refs
    i = pl.program_id(1)

    @pl.when(i == 0)
    def _():
        wbf_ref[...] = w_ref[...].astype(jnp.bfloat16)

    def emit(x_ref):
        acc = jnp.dot(x_ref[...], wbf_ref[...], preferred_element_type=jnp.float32)
        if has_res:
            acc = res_ref[...] + gate_ref[...] * acc
        o_ref[...] = acc.astype(o_ref.dtype)

    if split_tile is None:
        emit(x_refs[0])
    else:
        pl.when(i < split_tile)(lambda: emit(x_refs[0]))
        pl.when(i >= split_tile)(lambda: emit(x_refs[1]))


def _matmul(x, w, *, layer=None, col0=0, n=None, tm=1024, tn=512, out_dtype=jnp.bfloat16,
            res=None, mods=None, gate_chunk=None, single_buffer_w=False):
    xs = x if isinstance(x, (tuple, list)) else (x,)
    m, k = sum(a.shape[0] for a in xs), xs[0].shape[1]
    if w.ndim == 2:
        w = w[None]
        layer = 0
    n = w.shape[2] - col0 if n is None else n
    tm = min(tm, m)
    assert all(a.shape[0] % tm == 0 for a in xs) and col0 % tn == 0 and w.shape[1] == k
    cb0 = col0 // tn
    grid = (pl.cdiv(n, tn), m // tm)
    w_mode = {"pipeline_mode": pl.Buffered(1)} if single_buffer_w else {}
    if len(xs) == 1:
        split_tile = None
        in_specs = [pl.BlockSpec((tm, k), lambda j, i: (i, 0))]
    else:
        split_tile = xs[0].shape[0] // tm
        in_specs = [pl.BlockSpec((tm, k), lambda j, i: (jnp.minimum(i, split_tile - 1), 0)),
                    pl.BlockSpec((tm, k), lambda j, i: (jnp.maximum(i - split_tile, 0), 0))]
    in_specs.append(pl.BlockSpec((None, k, tn), lambda j, i: (layer, 0, cb0 + j), **w_mode))
    args = [*xs, w]
    has_res = res is not None
    if has_res:
        in_specs += [
            pl.BlockSpec((tm, tn), lambda j, i: (i, j)),
            pl.BlockSpec((None, None, 1, tn), lambda j, i: (_row_group(i, tm), gate_chunk, 0, j)),
        ]
        args += [res, mods]
    n_out = grid[0] * tn
    return pl.pallas_call(
        functools.partial(_mm_kernel, has_res=has_res, split_tile=split_tile),
        name=f"mm_k{k}_n{n}",
        grid=grid,
        in_specs=in_specs,
        out_specs=pl.BlockSpec((tm, tn), lambda j, i: (i, j)),
        out_shape=jax.ShapeDtypeStruct((m, n_out), out_dtype),
        scratch_shapes=[pltpu.VMEM((k, tn), jnp.bfloat16)],
        compiler_params=_params("parallel", "arbitrary"),
    )(*args)


HI16 = 0xFFFF0000


def _pack_rows(y):
    u = lax.bitcast_convert_type(y.astype(jnp.bfloat16).astype(jnp.float32), jnp.uint32)
    half = y.shape[1] // 2
    return (u[:, half:] & jnp.uint32(HI16)) | (u[:, :half] >> 16)


def _unpack_rows(u):
    lo = lax.bitcast_convert_type(u << 16, jnp.float32)
    hi = lax.bitcast_convert_type(u & jnp.uint32(HI16), jnp.float32)
    return jnp.concatenate([lo, hi], axis=1)


def _norm_kernel(*refs, has_mod, has_router):
    it = iter(refs)
    x_ref, w_ref = next(it), next(it)
    shift_ref = scale_ref = wr_ref = None
    if has_mod:
        shift_ref, scale_ref = next(it), next(it)
    if has_router:
        wr_ref = next(it)
    o_ref = next(it)
    x = x_ref[...].astype(jnp.float32)
    y = x * lax.rsqrt(jnp.mean(x * x, axis=-1, keepdims=True) + EPS) * w_ref[...]
    if has_mod:
        y = y * (1.0 + scale_ref[...]) + shift_ref[...]
    o_ref[...] = _pack_rows(y) if o_ref.dtype == jnp.uint32 else y.astype(o_ref.dtype)
    if has_router:
        logits_ref = next(it)
        logits_ref[...] = jnp.dot(y, wr_ref[...], preferred_element_type=jnp.float32,
                                  precision=lax.Precision.HIGHEST)


def _norm(x, w, *, mods=None, shift_chunk=None, scale_chunk=None, router_w=None, tm=256,
          out_dtype=jnp.bfloat16):
    m, d = x.shape
    has_mod = mods is not None
    has_router = router_w is not None
    in_specs = [pl.BlockSpec((tm, d), lambda i: (i, 0)), pl.BlockSpec((1, d), lambda i: (0, 0))]
    args = [x, w.reshape(1, d)]
    if has_mod:
        in_specs += [
            pl.BlockSpec((None, None, 1, d), lambda i: (_row_group(i, tm), shift_chunk, 0, 0)),
            pl.BlockSpec((None, None, 1, d), lambda i: (_row_group(i, tm), scale_chunk, 0, 0)),
        ]
        args += [mods, mods]
    d_out = d // 2 if out_dtype == jnp.uint32 else d
    out_specs = pl.BlockSpec((tm, d_out), lambda i: (i, 0))
    out_shape = jax.ShapeDtypeStruct((m, d_out), out_dtype)
    if has_router:
        in_specs.append(pl.BlockSpec((d, ROUTER_LANES), lambda i: (0, 0)))
        args.append(router_w)
        out_specs = [out_specs, pl.BlockSpec((tm, ROUTER_LANES), lambda i: (i, 0))]
        out_shape = [out_shape, jax.ShapeDtypeStruct((m, ROUTER_LANES), jnp.float32)]
    return pl.pallas_call(
        functools.partial(_norm_kernel, has_mod=has_mod, has_router=has_router),
        name="norm_router" if has_router else "norm",
        grid=(m // tm,),
        in_specs=in_specs,
        out_specs=out_specs,
        out_shape=out_shape,
        compiler_params=_params("parallel"),
    )(*args)


def _combine_kernel(x_ref, y_ref, g_ref, gate_ref, o_ref):
    half = y_ref.shape[1] // 2
    g = g_ref[...]
    y = g[:, 0:1] * _unpack_rows(y_ref[:, :half]) + g[:, 1:2] * _unpack_rows(y_ref[:, half:])
    o_ref[...] = x_ref[...] + gate_ref[...] * y


def _moe_combine(x, y_assign, gates, mods, gate_chunk, tm=256):
    m, d = x.shape
    y_tok = y_assign.reshape(m, MOE_TOP_K * (d // 2))
    return pl.pallas_call(
        _combine_kernel,
        name="moe_combine",
        grid=(m // tm,),
        in_specs=[
            pl.BlockSpec((tm, d), lambda i: (i, 0)),
            pl.BlockSpec((tm, y_tok.shape[1]), lambda i: (i, 0)),
            pl.BlockSpec((tm, MOE_TOP_K), lambda i: (i, 0)),
            pl.BlockSpec((None, None, 1, d), lambda i: (_row_group(i, tm), gate_chunk, 0, 0)),
        ],
        out_specs=pl.BlockSpec((tm, d), lambda i: (i, 0)),
        out_shape=jax.ShapeDtypeStruct((m, d), jnp.float32),
        compiler_params=_params("parallel"),
    )(x, y_tok, gates, mods)


def _moe_kernel(be_ref, first_ref, rows_ref, next_ref, tok_ref, asg_ref,
                h_hbm, wg_hbm, wu_hbm, wd_hbm, y_hbm,
                xbuf, obuf, stage_g, stage_u, stage_d, wg_bf, wu_bf, wd_bf, w_sems, g_sems, s_sems, *, layer):
    i = pl.program_id(0)
    n_blocks = pl.num_programs(0)
    slot = i % 2

    def weight_copies(e):
        return (pltpu.make_async_copy(wg_hbm.at[layer, e], stage_g, w_sems.at[0]),
                pltpu.make_async_copy(wu_hbm.at[layer, e], stage_u, w_sems.at[1]),
                pltpu.make_async_copy(wd_hbm.at[layer, e], stage_d, w_sems.at[2]))

    def gather_copy(blk, s, r):
        return pltpu.make_async_copy(h_hbm.at[pl.ds(tok_ref[blk * MOE_BLOCK + r], 1)],
                                     xbuf.at[s, pl.ds(r, 1)], g_sems.at[s])

    def scatter_copy(blk, s, r):
        return pltpu.make_async_copy(obuf.at[s, pl.ds(r, 1)],
                                     y_hbm.at[pl.ds(asg_ref[blk * MOE_BLOCK + r], 1)], s_sems.at[s])

    def for_rows(blk, fn):
        def body(r, carry):
            fn(r)
            return carry
        lax.fori_loop(0, rows_ref[blk], body, 0)

    @pl.when(i == 0)
    def _():
        xbuf[...] = jnp.zeros_like(xbuf)
        for_rows(0, lambda r: gather_copy(0, 0, r).start())
        for cp in weight_copies(be_ref[0]):
            cp.start(priority=1)

    @pl.when(i + 1 < n_blocks)
    def _():
        for_rows(i + 1, lambda r: gather_copy(i + 1, 1 - slot, r).start())

    @pl.when(first_ref[i] == 1)
    def _():
        for cp in weight_copies(be_ref[i]):
            cp.wait()
        wg_bf[...] = stage_g[...].astype(jnp.bfloat16)
        wu_bf[...] = stage_u[...].astype(jnp.bfloat16)
        wd_bf[...] = stage_d[...].astype(jnp.bfloat16)

        @pl.when(next_ref[i] >= 0)
        def _():
            for cp in weight_copies(next_ref[i]):
                cp.start(priority=1)

    @pl.when(i >= 2)
    def _():
        for_rows(i - 2, lambda r: scatter_copy(i - 2, slot, r).wait())

    @pl.when(rows_ref[i] > 0)
    def _():
        for_rows(i, lambda r: gather_copy(i, slot, r).wait())
        x = _unpack_rows(xbuf[slot]).astype(jnp.bfloat16)
        g = jnp.dot(x, wg_bf[...], preferred_element_type=jnp.float32)
        u = jnp.dot(x, wu_bf[...], preferred_element_type=jnp.float32)
        a = (g * jax.nn.sigmoid(g) * u).astype(jnp.bfloat16)
        obuf[slot] = _pack_rows(jnp.dot(a, wd_bf[...], preferred_element_type=jnp.float32))
        for_rows(i, lambda r: scatter_copy(i, slot, r).start())

    @pl.when(i == n_blocks - 1)
    def _():
        @pl.when(i >= 1)
        def _():
            for_rows(i - 1, lambda r: scatter_copy(i - 1, 1 - slot, r).wait())
        for_rows(i, lambda r: scatter_copy(i, slot, r).wait())


def _moe_ffn(h_packed, layout, w_gate, w_up, w_down, layer):
    block_expert, block_first, block_rows, block_next, row_token, row_assign = layout
    t, d2 = h_packed.shape
    d = 2 * d2
    n_blocks = block_expert.shape[0]
    grid_spec = pltpu.PrefetchScalarGridSpec(
        num_scalar_prefetch=6,
        grid=(n_blocks,),
        in_specs=[_any_spec(), _any_spec(), _any_spec(), _any_spec()],
        out_specs=_any_spec(),
        scratch_shapes=[
            pltpu.VMEM((2, MOE_BLOCK, d2), jnp.uint32),
            pltpu.VMEM((2, MOE_BLOCK, d2), jnp.uint32),
            pltpu.VMEM((d, MOE_FF), jnp.float32),
            pltpu.VMEM((d, MOE_FF), jnp.float32),
            pltpu.VMEM((MOE_FF, d), jnp.float32),
            pltpu.VMEM((d, MOE_FF), jnp.bfloat16),
            pltpu.VMEM((d, MOE_FF), jnp.bfloat16),
            pltpu.VMEM((MOE_FF, d), jnp.bfloat16),
            pltpu.SemaphoreType.DMA((3,)),
            pltpu.SemaphoreType.DMA((2,)),
            pltpu.SemaphoreType.DMA((2,)),
        ],
    )
    return pl.pallas_call(
        functools.partial(_moe_kernel, layer=layer),
        name="moe_ffn",
        grid_spec=grid_spec,
        out_shape=jax.ShapeDtypeStruct((t * MOE_TOP_K, d2), jnp.uint32),
        compiler_params=_params("arbitrary"),
    )(block_expert, block_first, block_rows, block_next, row_token, row_assign,
      h_packed, w_gate, w_up, w_down)


def _route(logits):
    t = logits.shape[0]
    g_logits = logits[:, :MOE_GROUPS]
    g_sel = jnp.argmax(g_logits, axis=-1)
    g_prob = jnp.take_along_axis(jax.nn.softmax(g_logits, axis=-1), g_sel[:, None], axis=1)
    e_logits = logits[:, MOE_GROUPS:MOE_GROUPS + N_EXPERTS].reshape(t, MOE_GROUPS, MOE_PER_GROUP)
    e_in_group = jnp.take_along_axis(e_logits, g_sel[:, None, None], axis=1)[:, 0]
    top_val, top_idx = lax.top_k(e_in_group, MOE_TOP_K)
    gates = g_prob * jax.nn.softmax(top_val, axis=-1)
    expert = (g_sel[:, None] * MOE_PER_GROUP + top_idx).astype(jnp.int32)
    return gates, expert


def _moe_layout(expert):
    n_assign = expert.size
    e_flat = expert.reshape(-1)
    onehot = (e_flat[:, None] == jnp.arange(N_EXPERTS, dtype=jnp.int32)[None, :]).astype(jnp.int32)
    csum = jnp.cumsum(onehot, axis=0)
    rank = jnp.take_along_axis(csum, e_flat[:, None], axis=1)[:, 0] - 1
    counts = csum[-1]
    padded = (counts + MOE_BLOCK - 1) // MOE_BLOCK * MOE_BLOCK
    p_ends = jnp.cumsum(padded)
    p_starts = p_ends - padded
    pos = (p_starts[e_flat] + rank).astype(jnp.int32)
    n_blocks = -(-n_assign // MOE_BLOCK) + N_EXPERTS
    n_rows = n_blocks * MOE_BLOCK
    assign = jnp.arange(n_assign, dtype=jnp.int32)
    row_assign = jnp.zeros((n_rows,), jnp.int32).at[pos].set(assign)
    row_token = row_assign // MOE_TOP_K
    block_start = jnp.arange(n_blocks, dtype=jnp.int32) * MOE_BLOCK
    block_expert = jnp.minimum(jnp.searchsorted(p_ends, block_start, side='right'),
                               N_EXPERTS - 1).astype(jnp.int32)
    valid = block_start < p_ends[-1]
    real_end = (p_starts + counts)[block_expert]
    block_rows = jnp.where(valid, jnp.clip(real_end - block_start, 0, MOE_BLOCK), 0).astype(jnp.int32)
    prev = jnp.concatenate([jnp.full((1,), -1, jnp.int32), block_expert[:-1]])
    block_first = (valid & (block_expert != prev)).astype(jnp.int32)
    after = jnp.minimum(p_ends[block_expert] // MOE_BLOCK, n_blocks - 1)
    has_next = p_ends[block_expert] < p_ends[-1]
    block_next = jnp.where(has_next, block_expert[after], -1).astype(jnp.int32)
    return block_expert, block_first, block_rows, block_next, row_token, row_assign


def _hier_moe(x, h_packed, logits, mods, gate_chunk, w_gate, w_up, w_down, layer):
    gates, expert = _route(logits)
    y_assign = _moe_ffn(h_packed, _moe_layout(expert), w_gate, w_up, w_down, layer)
    return _moe_combine(x, y_assign, gates, mods, gate_chunk)


def _log_sigmoid(x):
    return jnp.minimum(x, 0.0) - jnp.log(1.0 + jnp.exp(-jnp.abs(x)))


def _softplus(x):
    return jnp.maximum(x, 0.0) + jnp.log(1.0 + jnp.exp(-jnp.abs(x)))


def _mlstm_kernel(*refs, seq, has_init, emit_state, n_alias, state_slot):
    f32, bf16 = jnp.float32, jnp.bfloat16
    it = iter(refs)
    q_ref, k_ref, v_ref, o_ref, gc_ref, gr_ref, nw_ref = (next(it) for _ in range(7))
    if has_init:
        c0_ref, n0_ref, m0_ref = next(it), next(it), next(it)
    for _ in range(n_alias):
        next(it)
    y_ref = next(it)
    if emit_state:
        cout_ref, nout_ref, mout_ref = next(it), next(it), next(it)
    hacc, c_sc, n_sc, m_sc = next(it), next(it), next(it), next(it)

    L = MLSTM_CHUNK
    nc = seq // L
    use_inter = has_init or nc > 1
    scale = MLSTM_DQK ** -0.5
    row = lax.broadcasted_iota(jnp.int32, (L, L), 0)
    col = lax.broadcasted_iota(jnp.int32, (L, L), 1)

    for d in range(2):
        tri = (col <= row) if d == 0 else (col >= row)
        tri_t = (row <= col) if d == 0 else (row >= col)
        if has_init:
            c_sc[...] = c0_ref[d]
            n_sc[...] = n0_ref[d]
            m_sc[...] = m0_ref[d]
        else:
            c_sc[...] = jnp.zeros_like(c_sc)
            n_sc[...] = jnp.zeros_like(n_sc)
            m_sc[...] = jnp.zeros_like(m_sc)

        def chunk(ci, carry, d=d, tri=tri, tri_t=tri_t):
            c = ci if d == 0 else nc - 1 - ci
            rows = _chunk_rows(c, L)
            q, k, v = q_ref[rows, :], k_ref[rows, :], v_ref[rows, :]
            gcol, grow = gc_ref[c], gr_ref[c]
            i_col = gcol[:, 2 * d:2 * d + 1]
            f_col = _log_sigmoid(gcol[:, 2 * d + 1:2 * d + 2])
            i_row = grow[2 * d:2 * d + 1, :]
            f_row = _log_sigmoid(grow[2 * d + 1:2 * d + 2, :])
            b_col = jnp.sum(jnp.where(tri, f_row, 0.0), axis=1, keepdims=True)
            b_row = jnp.sum(jnp.where(tri_t, f_col, 0.0), axis=0, keepdims=True)
            b_end = jnp.sum(f_row, axis=1, keepdims=True)
            m_prev = m_sc[...]
            log_d = jnp.where(tri, b_col - b_row + i_row, -jnp.inf)
            log_inter = b_col + m_prev
            m_t = jnp.maximum(log_inter, jnp.max(log_d, axis=1, keepdims=True))
            w_intra = jnp.exp(log_d - m_t)
            qk = lax.dot_general(q, k, (((1,), (1,)), ((), ())), preferred_element_type=f32) * (scale * w_intra)
            num = jnp.dot(qk.astype(bf16), v, preferred_element_type=f32)
            den = jnp.sum(qk, axis=1, keepdims=True)
            if use_inter:
                w_inter = jnp.exp(log_inter - m_t) * scale
                num = num + w_inter * jnp.dot(q, c_sc[...].astype(bf16), preferred_element_type=f32)
                den = den + w_inter * jnp.sum(q.astype(f32) * n_sc[...], axis=1, keepdims=True)
            h = num / jnp.maximum(jnp.abs(den), jnp.exp(-m_t))
            if d == 0:
                hacc[rows, :] = h
            else:
                hacc[rows, :] = hacc[rows, :] + h
            if emit_state or nc > 1:
                log_w = b_end - b_col + i_col
                m_new = jnp.maximum(b_end + m_prev, jnp.max(log_w, axis=0, keepdims=True))
                kw = k.astype(f32) * jnp.exp(log_w - m_new)
                decay = jnp.exp(b_end + m_prev - m_new)
                c_sc[...] = decay * c_sc[...] + jnp.dot(kw.T.astype(bf16), v, preferred_element_type=f32)
                n_sc[...] = decay * n_sc[...] + jnp.sum(kw, axis=0, keepdims=True)
                m_sc[...] = m_new
            return carry

        if nc == 1:
            chunk(0, 0)
        else:
            lax.fori_loop(0, nc, chunk, 0)
        if emit_state and state_slot is None:
            cout_ref[d] = c_sc[...]
            nout_ref[d] = n_sc[...]
            mout_ref[d] = m_sc[...]
        elif emit_state:
            for slot in range(cout_ref.shape[0]):
                if slot == state_slot:
                    cout_ref[slot, d] = c_sc[...]
                    nout_ref[slot, d] = n_sc[...]
                    mout_ref[slot, d] = m_sc[...]
                else:
                    cout_ref[slot, d] = jnp.zeros_like(c_sc)
                    nout_ref[slot, d] = jnp.zeros_like(n_sc)
                    mout_ref[slot, d] = jnp.zeros_like(m_sc)

    hs = hacc[...]
    hn = hs * lax.rsqrt(jnp.mean(hs * hs, axis=-1, keepdims=True) + EPS) * nw_ref[...]
    y_ref[...] = (hn * jax.nn.sigmoid(o_ref[...].astype(f32))).astype(y_ref.dtype)


def _mlstm_seq(proj, gates, norm_w, layer, *, row0, nb, seq, init=None, state_prev=None,
               emit_state=False, n_layers=1):
    f32 = jnp.float32
    H, L = MLSTM_HEADS, MLSTM_CHUNK
    nc = seq // L
    rb0 = row0 // seq
    g = gates[row0:row0 + nb * seq].reshape(nb, nc, L, 2, 2, H)
    g_col = g.transpose(0, 5, 1, 2, 3, 4).reshape(nb, H, nc, L, 4)
    g_row = g.transpose(0, 5, 1, 3, 4, 2).reshape(nb, H, nc, 4, L)
    qb, vb = MLSTM_DQK, MLSTM_DV
    in_specs = [
        pl.BlockSpec((seq, qb), lambda b, h: (rb0 + b, h)),
        pl.BlockSpec((seq, qb), lambda b, h: (rb0 + b, H + h)),
        pl.BlockSpec((seq, vb), lambda b, h: (rb0 + b, H + h)),
        pl.BlockSpec((seq, vb), lambda b, h: (rb0 + b, 2 * H + h)),
        pl.BlockSpec((None, None, nc, L, 4), lambda b, h: (b, h, 0, 0, 0)),
        pl.BlockSpec((None, None, nc, 4, L), lambda b, h: (b, h, 0, 0, 0)),
        pl.BlockSpec((None, None, 1, vb), lambda b, h: (layer, h, 0, 0)),
    ]
    args = [proj, proj, proj, proj, g_col, g_row, norm_w.reshape(-1, H, 1, vb)]
    has_init = init is not None
    if has_init:
        c0, n0, m0 = init
        in_specs += [
            pl.BlockSpec((None, None, 2, None, qb, vb), lambda b, h: (b, layer, 0, h, 0, 0)),
            pl.BlockSpec((None, None, 2, None, 1, qb), lambda b, h: (b, layer, 0, h, 0, 0)),
            pl.BlockSpec((None, None, 2, None, 1, 1), lambda b, h: (b, layer, 0, h, 0, 0)),
        ]
        args += [c0, n0.reshape(*n0.shape[:4], 1, qb), m0.reshape(*m0.shape[:4], 1, 1)]
    out_specs = [pl.BlockSpec((seq, vb), lambda b, h: (b, h))]
    out_shape = [jax.ShapeDtypeStruct((nb * seq, H * vb), jnp.bfloat16)]
    state_slot = None
    if emit_state and state_prev is None:
        state_slot = layer
        out_specs += [
            pl.BlockSpec((None, n_layers, 2, None, qb, vb), lambda b, h: (b, 0, 0, h, 0, 0)),
            pl.BlockSpec((None, n_layers, 2, None, 1, qb), lambda b, h: (b, 0, 0, h, 0, 0)),
            pl.BlockSpec((None, n_layers, 2, None, 1, 1), lambda b, h: (b, 0, 0, h, 0, 0)),
        ]
    elif emit_state:
        out_specs += [
            pl.BlockSpec((None, None, 2, None, qb, vb), lambda b, h: (b, layer, 0, h, 0, 0)),
            pl.BlockSpec((None, None, 2, None, 1, qb), lambda b, h: (b, layer, 0, h, 0, 0)),
            pl.BlockSpec((None, None, 2, None, 1, 1), lambda b, h: (b, layer, 0, h, 0, 0)),
        ]
    if emit_state:
        out_shape += [
            jax.ShapeDtypeStruct((nb, n_layers, 2, H, qb, vb), f32),
            jax.ShapeDtypeStruct((nb, n_layers, 2, H, 1, qb), f32),
            jax.ShapeDtypeStruct((nb, n_layers, 2, H, 1, 1), f32),
        ]
    aliases = {}
    n_alias = 0
    if state_prev is not None:
        for k_out, a in enumerate(state_prev):
            aliases[len(args)] = 1 + k_out
            in_specs.append(_any_spec())
            args.append(a)
            n_alias += 1
    return pl.pallas_call(
        functools.partial(_mlstm_kernel, seq=seq, has_init=has_init, emit_state=emit_state, n_alias=n_alias,
                          state_slot=state_slot),
        name=f"mlstm_s{seq}",
        grid=(nb, H),
        in_specs=in_specs,
        out_specs=out_specs,
        out_shape=out_shape,
        input_output_aliases=aliases,
        scratch_shapes=[
            pltpu.VMEM((seq, vb), f32),
            pltpu.VMEM((qb, vb), f32),
            pltpu.VMEM((1, qb), f32),
            pltpu.VMEM((1, 1), f32),
        ],
        compiler_params=_params("arbitrary", "arbitrary"),
    )(*args)


def _conv_silu(x, w_ref, b_ref):
    s = x.shape[0]
    t = lax.broadcasted_iota(jnp.int32, x.shape, 0)
    y = (w_ref[0:1, :] * jnp.where(t >= 2, pltpu.roll(x, 2, 0), 0.0)
         + w_ref[1:2, :] * jnp.where(t >= 1, pltpu.roll(x, 1, 0), 0.0)
         + w_ref[2:3, :] * x
         + w_ref[3:4, :] * jnp.where(t < s - 1, pltpu.roll(x, s - 1, 0), 0.0)
         + b_ref[...])
    return y * jax.nn.sigmoid(y)


def _ssd_kernel(*refs, seq, has_init, emit_state):
    f32, bf16 = jnp.float32, jnp.bfloat16
    it = iter(refs)
    (x_ref, b_ref, c_ref, z_ref, dtc_ref, dtr_ref, dtbc_ref, dtbr_ref, alc_ref, alr_ref,
     cwx_ref, cbx_ref, cwb_ref, cbb_ref, cwc_ref, cbc_ref, dsk_ref, nw_ref) = (next(it) for _ in range(18))
    if has_init:
        h0_ref = next(it)
    y_ref = next(it)
    if emit_state:
        hout_ref = next(it)
    xc_sc, bc_sc, cc_sc, yacc, hst = (next(it) for _ in range(5))

    L = SSD_CHUNK
    nc = seq // L
    E, P = SSD_GROUP_HEADS, SSD_HEAD_DIM
    n_pair = E // 2
    W = 2 * P

    for p in range(n_pair):
        cols = slice(p * W, (p + 1) * W)
        xc_sc[:, cols] = _conv_silu(x_ref[:, cols].astype(f32), cwx_ref.at[:, cols], cbx_ref.at[:, cols])
    bc_sc[...] = _conv_silu(b_ref[...].astype(f32), cwb_ref, cbb_ref).astype(bf16)
    cc_sc[...] = _conv_silu(c_ref[...].astype(f32), cwc_ref, cbc_ref).astype(bf16)

    row = lax.broadcasted_iota(jnp.int32, (L, L), 0)
    col = lax.broadcasted_iota(jnp.int32, (L, L), 1)
    lo = lax.broadcasted_iota(jnp.int32, (L, W), 1) < P
    lo_rows = lax.broadcasted_iota(jnp.int32, (W, SSD_STATE), 0) < P
    a_col = -jnp.exp(alc_ref[...])
    a_row = -jnp.exp(alr_ref[...])

    for d in range(2):
        tri = (col <= row) if d == 0 else (col >= row)
        tri_f = tri.astype(f32)
        tri_tf = ((row <= col) if d == 0 else (row >= col)).astype(f32)
        if has_init:
            hst[...] = h0_ref[d]
        else:
            hst[...] = jnp.zeros_like(hst)

        def chunk(ci, carry, d=d, tri=tri, tri_f=tri_f, tri_tf=tri_tf):
            c = ci if d == 0 else nc - 1 - ci
            rows = _chunk_rows(c, L)
            dt_c = _softplus(dtc_ref[c] + dtbc_ref[...])
            dt_r = _softplus(dtr_ref[c] + dtbr_ref[...])
            cs_c = jnp.dot(tri_f, dt_c * a_col, preferred_element_type=f32, precision=lax.Precision.HIGHEST)
            cs_r = jnp.dot(dt_r * a_row, tri_tf, preferred_element_type=f32, precision=lax.Precision.HIGHEST)
            cs_e = cs_c[L - 1:L, :] if d == 0 else cs_c[0:1, :]
            bm, cm = bc_sc[rows, :], cc_sc[rows, :]
            cb = lax.dot_general(cm, bm, (((1,), (1,)), ((), ())), preferred_element_type=f32)
            for p in range(n_pair):
                cols = slice(p * W, (p + 1) * W)
                ia, ib = d * E + 2 * p, d * E + 2 * p + 1
                xdt = xc_sc[rows, cols] * jnp.where(lo, dt_c[:, ia:ia + 1], dt_c[:, ib:ib + 1])
                xdt_bf = xdt.astype(bf16)
                cs_ta, cs_tb = cs_c[:, ia:ia + 1], cs_c[:, ib:ib + 1]
                wa = (cb * jnp.exp(jnp.where(tri, cs_ta - cs_r[ia:ia + 1, :], -jnp.inf))).astype(bf16)
                wb = (cb * jnp.exp(jnp.where(tri, cs_tb - cs_r[ib:ib + 1, :], -jnp.inf))).astype(bf16)
                y = jnp.where(lo, jnp.dot(wa, xdt_bf, preferred_element_type=f32),
                              jnp.dot(wb, xdt_bf, preferred_element_type=f32))
                hrows = slice(p * W, (p + 1) * W)
                hp = hst[hrows, :]
                if has_init or nc > 1:
                    y_inter = lax.dot_general(cm, hp.astype(bf16), (((1,), (1,)), ((), ())),
                                              preferred_element_type=f32)
                    y = y + jnp.where(lo, jnp.exp(cs_ta), jnp.exp(cs_tb)) * y_inter
                if d == 0:
                    yacc[rows, cols] = y
                else:
                    yacc[rows, cols] = yacc[rows, cols] + y
                if emit_state or nc > 1:
                    ea, eb = cs_e[:, ia:ia + 1], cs_e[:, ib:ib + 1]
                    xw = xdt * jnp.where(lo, jnp.exp(ea - cs_ta), jnp.exp(eb - cs_tb))
                    contrib = jnp.dot(xw.T.astype(bf16), bm, preferred_element_type=f32)
                    hst[hrows, :] = jnp.where(lo_rows, jnp.exp(ea), jnp.exp(eb)) * hp + contrib
            return carry

        if nc == 1:
            chunk(0, 0)
        else:
            lax.fori_loop(0, nc, chunk, 0)
        if emit_state:
            hout_ref[d] = hst[...]

    y = yacc[...] + dsk_ref[...] * xc_sc[...]
    z = z_ref[...].astype(f32)
    y = y * (z * jax.nn.sigmoid(z))
    y = y * lax.rsqrt(jnp.mean(y * y, axis=-1, keepdims=True) + EPS) * nw_ref[...]
    y_ref[...] = y.astype(y_ref.dtype)


def _ssd_seq(zx, dt_raw, conv_w, conv_b, dt_bias, a_log, d_skip, norm_w, layer, *, row0, nb, seq,
             init=None, emit_state=False):
    f32 = jnp.float32
    G, E, L, N = SSD_GROUPS, SSD_GROUP_HEADS, SSD_CHUNK, SSD_STATE
    gc = SSD_GROUP_COLS
    nc = seq // L
    rb0 = row0 // seq
    dt = dt_raw[row0:row0 + nb * seq].reshape(nb, nc, L, 2, G, E)
    dt_col = dt.transpose(0, 4, 1, 2, 3, 5).reshape(nb, G, nc, L, 2 * E)
    dt_row = dt.transpose(0, 4, 1, 3, 5, 2).reshape(nb, G, nc, 2 * E, L)

    def per_group(a):
        a = a.astype(f32).reshape(2, G, E).transpose(1, 0, 2).reshape(G, 2 * E)
        return a[:, None, :], a[:, :, None]

    dtb_c, dtb_r = per_group(dt_bias)
    al_c, al_r = per_group(a_log)
    x_cb = SSD_INNER // gc
    bc0 = 2 * SSD_INNER // N
    cw = conv_w.astype(f32)
    cbias = conv_b.astype(f32).reshape(1, -1)
    dsk = jnp.repeat(d_skip.astype(f32), SSD_HEAD_DIM).reshape(1, SSD_INNER)
    in_specs = [
        pl.BlockSpec((seq, gc), lambda b, g: (rb0 + b, x_cb + g)),
        pl.BlockSpec((seq, N), lambda b, g: (rb0 + b, bc0 + g)),
        pl.BlockSpec((seq, N), lambda b, g: (rb0 + b, bc0 + G + g)),
        pl.BlockSpec((seq, gc), lambda b, g: (rb0 + b, g)),
        pl.BlockSpec((None, None, nc, L, 2 * E), lambda b, g: (b, g, 0, 0, 0)),
        pl.BlockSpec((None, None, nc, 2 * E, L), lambda b, g: (b, g, 0, 0, 0)),
        pl.BlockSpec((None, 1, 2 * E), lambda b, g: (g, 0, 0)),
        pl.BlockSpec((None, 2 * E, 1), lambda b, g: (g, 0, 0)),
        pl.BlockSpec((None, 1, 2 * E), lambda b, g: (g, 0, 0)),
        pl.BlockSpec((None, 2 * E, 1), lambda b, g: (g, 0, 0)),
        pl.BlockSpec((SSD_CONV, gc), lambda b, g: (0, g)),
        pl.BlockSpec((1, gc), lambda b, g: (0, g)),
        pl.BlockSpec((SSD_CONV, N), lambda b, g: (0, SSD_INNER // N + g)),
        pl.BlockSpec((1, N), lambda b, g: (0, SSD_INNER // N + g)),
        pl.BlockSpec((SSD_CONV, N), lambda b, g: (0, SSD_INNER // N + G + g)),
        pl.BlockSpec((1, N), lambda b, g: (0, SSD_INNER // N + G + g)),
        pl.BlockSpec((1, gc), lambda b, g: (0, g)),
        pl.BlockSpec((1, gc), lambda b, g: (0, g)),
    ]
    args = [zx, zx, zx, zx, dt_col, dt_row, dtb_c, dtb_r, al_c, al_r,
            cw, cbias, cw, cbias, cw, cbias, dsk, norm_w.astype(f32).reshape(1, SSD_INNER)]
    has_init = init is not None
    state_block = pl.BlockSpec((None, None, 2, None, E * SSD_HEAD_DIM, N), lambda b, g: (b, layer, 0, g, 0, 0))
    if has_init:
        in_specs.append(state_block)
        args.append(init.reshape(*init.shape[:3], G, E * SSD_HEAD_DIM, N))
    out_specs = [pl.BlockSpec((seq, gc), lambda b, g: (b, g))]
    out_shape = [jax.ShapeDtypeStruct((nb * seq, SSD_INNER), jnp.bfloat16)]
    if emit_state:
        out_specs.append(state_block)
        out_shape.append(jax.ShapeDtypeStruct((nb, 1, 2, G, E * SSD_HEAD_DIM, N), f32))
    return pl.pallas_call(
        functools.partial(_ssd_kernel, seq=seq, has_init=has_init, emit_state=emit_state),
        name=f"ssd_s{seq}",
        grid=(nb, G),
        in_specs=in_specs,
        out_specs=out_specs,
        out_shape=out_shape,
        scratch_shapes=[
            pltpu.VMEM((seq, gc), f32),
            pltpu.VMEM((seq, N), jnp.bfloat16),
            pltpu.VMEM((seq, N), jnp.bfloat16),
            pltpu.VMEM((seq, gc), f32),
            pltpu.VMEM((E * SSD_HEAD_DIM, N), f32),
        ],
        compiler_params=_params("arbitrary", "arbitrary"),
    )(*args)


def _attn_kernel(*refs, heads, rope):
    f32, bf16 = jnp.float32, jnp.bfloat16
    if rope:
        qn_ref, qp_ref, kv_ref, kpe_ref, cos_ref, sin_ref, o_ref = refs
    else:
        qn_ref, qp_ref, kv_ref, kpe_ref, o_ref = refs
    scale = (MLA_NOPE + MLA_ROPE) ** -0.5
    kpe2 = kpe_ref[...]
    lq = qn_ref.shape[0]
    lane = lax.broadcasted_iota(jnp.int32, (lq, V7X_LANES), 1)
    nt = (((1,), (1,)), ((), ()))
    for pr in range(heads // 2):
        qp = qp_ref[:, pr * V7X_LANES:(pr + 1) * V7X_LANES]
        if rope:
            x = qp.astype(f32)
            other = jnp.where(lane % (2 * ROPE_FREQS) < ROPE_FREQS,
                              pltpu.roll(x, V7X_LANES - ROPE_FREQS, 1), pltpu.roll(x, ROPE_FREQS, 1))
            qp = (x * cos_ref[...] + other * sin_ref[...]).astype(bf16)
        for a in range(2):
            h = 2 * pr + a
            qpa = jnp.where((lane < MLA_ROPE) if a == 0 else (lane >= MLA_ROPE), qp, jnp.zeros_like(qp))
            kn = kv_ref[:, h * 2 * MLA_NOPE:h * 2 * MLA_NOPE + MLA_NOPE]
            vv = kv_ref[:, h * 2 * MLA_NOPE + MLA_NOPE:(h + 1) * 2 * MLA_NOPE]
            s = (lax.dot_general(qn_ref[:, h * MLA_NOPE:(h + 1) * MLA_NOPE], kn, nt, preferred_element_type=f32)
                 + lax.dot_general(qpa, kpe2, nt, preferred_element_type=f32)) * scale
            e = jnp.exp(s - jnp.max(s, axis=-1, keepdims=True))
            p = e / jnp.sum(e, axis=-1, keepdims=True)
            o_ref[:, h * MLA_V:(h + 1) * MLA_V] = jnp.dot(p.astype(bf16), vv,
                                                          preferred_element_type=f32).astype(o_ref.dtype)


def _attention(qn, qp, kv, kpe2, *, row0, nb, seq, key0, n_keys, q_tile, heads, tables=None):
    nq = seq // q_tile
    qb0 = row0 // q_tile
    kb0 = key0 // n_keys
    assert row0 % q_tile == 0 and key0 % n_keys == 0
    hb = MLA_HEADS // heads
    in_specs = [
        pl.BlockSpec((q_tile, heads * MLA_NOPE), lambda b, t, h: (qb0 + b * nq + t, h)),
        pl.BlockSpec((q_tile, heads * MLA_ROPE), lambda b, t, h: (qb0 + b * nq + t, h)),
        pl.BlockSpec((n_keys, heads * 2 * MLA_NOPE), lambda b, t, h: (kb0 + b, h)),
        pl.BlockSpec((n_keys, V7X_LANES), lambda b, t, h: (kb0 + b, 0)),
    ]
    args = [qn, qp, kv, kpe2]
    rope = tables is not None
    if rope:
        in_specs += [pl.BlockSpec((q_tile, V7X_LANES), lambda b, t, h: (t, 0))] * 2
        args += list(tables)
    return pl.pallas_call(
        functools.partial(_attn_kernel, heads=heads, rope=rope),
        name=f"attn_s{seq}",
        grid=(nb, nq, hb),
        in_specs=in_specs,
        out_specs=pl.BlockSpec((q_tile, heads * MLA_V), lambda b, t, h: (b * nq + t, h)),
        out_shape=jax.ShapeDtypeStruct((nb * seq, MLA_HEADS * MLA_V), jnp.bfloat16),
        compiler_params=_params("arbitrary", "arbitrary", "arbitrary"),
    )(*args)


def _rope_tables(n_tok):
    rows = n_tok // GRID_W
    row = jnp.repeat(jnp.arange(rows, dtype=jnp.float32), GRID_W)
    col = jnp.tile(jnp.arange(GRID_W, dtype=jnp.float32), rows)
    inv_freq = ROPE_THETA ** (-jnp.arange(ROPE_FREQS, dtype=jnp.float32) / ROPE_FREQS)
    ang = jnp.stack([row, col], axis=-1)[:, :, None] * inv_freq
    cos, sin = jnp.cos(ang), jnp.sin(ang)
    cos64 = jnp.concatenate([cos[:, 0], cos[:, 0], cos[:, 1], cos[:, 1]], axis=-1)
    sin64 = jnp.concatenate([-sin[:, 0], sin[:, 0], -sin[:, 1], sin[:, 1]], axis=-1)
    return jnp.tile(cos64, (1, 2)), jnp.tile(sin64, (1, 2))


def _rope_rows(x, cos64, sin64):
    xr = x.reshape(x.shape[0], 2, 2, ROPE_FREQS)
    other = jnp.stack([xr[:, :, 1], xr[:, :, 0]], axis=2).reshape(x.shape)
    return x * cos64 + other * sin64


def kernel(x_prompt, x_sample, state_mlstm_C, state_mlstm_n, state_mlstm_m, state_ssm, cache_mla_ckv, cache_mla_kpe, c, c_ctx, ada_w, ada_b, norm_mix, norm_ffn, mlstm_w_in, mlstm_b_gates, mlstm_norm, mlstm_w_out, ssd_w_in, ssd_conv_w, ssd_conv_b, ssd_dt_bias, ssd_A_log, ssd_D, ssd_norm, ssd_w_out, mla_w_in, mla_q_norm, mla_kv_norm, mla_w_qb, mla_w_kvb, mla_w_out, router_group_w, router_expert_w, moe_w_gate, moe_w_up, moe_w_down, final_norm):
    f32, bf16 = jnp.float32, jnp.bfloat16
    D = D_MODEL
    X = jnp.concatenate([x_prompt.reshape(N_PROMPT, D), x_sample.reshape(N_SAMPLE, D)], axis=0)

    cond = jnp.concatenate([c_ctx[None, :], c], axis=0)
    cond = jnp.pad(jax.nn.silu(cond), ((0, 16 - cond.shape[0]), (0, 0))).astype(bf16)

    n_mlstm = mlstm_w_in.shape[0]
    mlstm_state = None
    new_ssm, new_ckv, new_kpe = [], [], []
    for l in range(DEPTH):
        kind, j = l % N_MIXERS, l // N_MIXERS
        ada = _matmul(cond, ada_w, layer=l, tm=16, tn=512, out_dtype=f32)
        mods = (ada[:1 + DEC_BATCH] + ada_b[l]).reshape(1 + DEC_BATCH, N_MOD, 1, D)
        h = _norm(X, norm_mix[l], mods=mods, shift_chunk=0, scale_chunk=1)

        if kind == 0:
            proj = _matmul(h, mlstm_w_in, layer=j, n=12288)
            n_gate = 4 * MLSTM_HEADS
            gates = _matmul(h, mlstm_w_in[j, :, 12288:], tn=n_gate, out_dtype=f32) + mlstm_b_gates[j]
            outs = _mlstm_seq(proj, gates, mlstm_norm, j, row0=0, nb=BATCH, seq=SEQ, emit_state=True,
                              state_prev=mlstm_state, n_layers=n_mlstm)
            y_p, mlstm_state = outs[0], tuple(outs[1:])
            y_s = _mlstm_seq(proj, gates, mlstm_norm, j, row0=N_PROMPT, nb=DEC_BATCH, seq=DEC_SEQ,
                             init=(state_mlstm_C, state_mlstm_n, state_mlstm_m))[0]
            X = _matmul((y_p, y_s), mlstm_w_out, layer=j, tm=512, out_dtype=f32, res=X, mods=mods, gate_chunk=2)
        elif kind == 1:
            zx = _matmul(h, ssd_w_in, layer=j, n=18432)
            dt_raw = _matmul(h, ssd_w_in, layer=j, col0=18432, n=256, tn=256, out_dtype=f32)
            w = (ssd_conv_w[j], ssd_conv_b[j], ssd_dt_bias[j], ssd_A_log[j], ssd_D[j], ssd_norm[j])
            y_p, s_h = _ssd_seq(zx, dt_raw, *w, 0, row0=0, nb=BATCH, seq=SEQ, emit_state=True)
            y_s = _ssd_seq(zx, dt_raw, *w, j, row0=N_PROMPT, nb=DEC_BATCH, seq=DEC_SEQ, init=state_ssm)[0]
            new_ssm.append(s_h.reshape(BATCH, 2, SSD_HEADS, SSD_HEAD_DIM, SSD_STATE))
            X = _matmul(jnp.concatenate([y_p, y_s], axis=0), ssd_w_out, layer=j, tm=512, tn=512, out_dtype=f32, res=X, mods=mods, gate_chunk=2,
                        single_buffer_w=True)
        else:
            lat = _matmul(h, mla_w_in, layer=j, n=2048, out_dtype=f32)
            kpe = _matmul(h, mla_w_in[j, :, 2048:], tn=MLA_ROPE, out_dtype=f32)
            hq = _norm(lat[:, :MLA_Q_RANK], mla_q_norm[j])
            ckv = _norm(lat[:, MLA_Q_RANK:], mla_kv_norm[j], out_dtype=f32)
            wq = mla_w_qb[j].reshape(MLA_Q_RANK, MLA_HEADS, MLA_NOPE + MLA_ROPE)
            qn = _matmul(hq, wq[:, :, :MLA_NOPE].reshape(MLA_Q_RANK, -1), tn=512)
            qp = _matmul(hq, wq[:, :, MLA_NOPE:].reshape(MLA_Q_RANK, -1), tn=512)
            n_keys = PAST_LEN + DEC_SEQ
            n_skeys = DEC_BATCH * n_keys
            cos2, sin2 = _rope_tables(DEC_SEQ)
            ckv_s = jnp.concatenate([cache_mla_ckv[:, j], ckv[N_PROMPT:].reshape(DEC_BATCH, DEC_SEQ, -1)], axis=1)
            ckv_all = jnp.concatenate([ckv_s.reshape(-1, MLA_KV_RANK), ckv[:N_PROMPT]], axis=0).astype(bf16)
            kv = _matmul(ckv_all, mla_w_kvb, layer=j, tn=512)
            kpe_lat = _rope_rows(kpe[N_PROMPT:], jnp.tile(cos2[:, :MLA_ROPE], (DEC_BATCH, 1)),
                                 jnp.tile(sin2[:, :MLA_ROPE], (DEC_BATCH, 1)))
            kpe_s = jnp.concatenate([cache_mla_kpe[:, j], kpe_lat.reshape(DEC_BATCH, DEC_SEQ, -1)], axis=1)
            kpe_all = jnp.concatenate([kpe_s.reshape(-1, MLA_ROPE), kpe[:N_PROMPT]], axis=0).astype(bf16)
            kpe2 = jnp.concatenate([kpe_all, kpe_all], axis=1)
            o_p = _attention(qn, qp, kv, kpe2, row0=0, nb=BATCH, seq=SEQ, key0=n_skeys, n_keys=SEQ, q_tile=SEQ,
                             heads=8)
            o_s = _attention(qn, qp, kv, kpe2, row0=N_PROMPT, nb=DEC_BATCH, seq=DEC_SEQ, key0=0,
                             n_keys=n_keys, q_tile=ATTN_Q_TILE, heads=2, tables=(cos2, sin2))
            new_ckv.append(ckv[:N_PROMPT].reshape(BATCH, SEQ, MLA_KV_RANK))
            new_kpe.append(kpe[:N_PROMPT].reshape(BATCH, SEQ, MLA_ROPE))
            X = _matmul(jnp.concatenate([o_p, o_s], axis=0), mla_w_out, layer=j, tm=512, tn=512, out_dtype=f32, res=X, mods=mods, gate_chunk=2,
                        single_buffer_w=True)

        router_w = jnp.pad(jnp.concatenate([router_group_w[l], router_expert_w[l]], axis=1),
                           ((0, 0), (0, ROUTER_LANES - MOE_GROUPS - N_EXPERTS)))
        hf, logits = _norm(X, norm_ffn[l], mods=mods, shift_chunk=3, scale_chunk=4, router_w=router_w,
                           out_dtype=jnp.uint32)
        X = _hier_moe(X, hf, logits, mods, 5, moe_w_gate, moe_w_up, moe_w_down, l)

    Y = _norm(X, final_norm, out_dtype=f32)
    y_prompt = Y[:N_PROMPT].reshape(BATCH, SEQ, D)
    y_sample = Y[N_PROMPT:].reshape(DEC_BATCH, DEC_SEQ, D)
    new_c, new_n, new_m = mlstm_state
    return (y_prompt, y_sample, new_c,
            new_n.reshape(BATCH, n_mlstm, 2, MLSTM_HEADS, MLSTM_DQK),
            new_m.reshape(BATCH, n_mlstm, 2, MLSTM_HEADS),
            jnp.stack(new_ssm, axis=1), jnp.stack(new_ckv, axis=1), jnp.stack(new_kpe, axis=1))
```

```python
import functools

import jax
import jax.numpy as jnp
from jax import lax
from jax.experimental import pallas as pl
from jax.experimental.pallas import tpu as pltpu

D_MODEL = 4096
BATCH = 32
SEQ = 256
DEPTH = 4
DEC_BATCH = 2
DEC_SEQ = 1024
PAST_LEN = 512
GRID_W = 64
N_MIXERS = 3
EPS = 1e-6

MLSTM_HEADS = 8
MLSTM_DQK = 256
MLSTM_DV = 512

SSD_INNER = 8192
SSD_HEAD_DIM = 64
SSD_HEADS = 128
SSD_GROUPS = 8
SSD_STATE = 128
SSD_CONV = 4
SSD_GROUP_HEADS = SSD_HEADS // SSD_GROUPS
SSD_GROUP_COLS = SSD_INNER // SSD_GROUPS

MLA_HEADS = 64
MLA_Q_RANK = 1536
MLA_KV_RANK = 512
MLA_NOPE = 128
MLA_ROPE = 64
MLA_V = 128
ROPE_THETA = 10000.0
ROPE_FREQS = 16

MOE_GROUPS = 8
MOE_PER_GROUP = 8
N_EXPERTS = 64
MOE_TOP_K = 2
MOE_FF = 512

N_PROMPT = BATCH * SEQ
N_SAMPLE = DEC_BATCH * DEC_SEQ
N_TOK = N_PROMPT + N_SAMPLE
N_MOD = 6

V7X_VMEM_LIMIT = 56 * 1024 * 1024
V7X_LANES = 128
MOE_BLOCK = 256
DMA_UNROLL = 8
ROUTER_LANES = V7X_LANES
MLSTM_CHUNK = 256
SSD_CHUNK = 128
ATTN_Q_TILE = 512


def _params(*sem):
    return pltpu.CompilerParams(dimension_semantics=sem, vmem_limit_bytes=V7X_VMEM_LIMIT)


def _row_group(tile, rows_per_tile):
    first_sample = N_PROMPT // rows_per_tile
    per_batch = DEC_SEQ // rows_per_tile
    return jnp.where(tile < first_sample, 0, 1 + (tile - first_sample) // per_batch)


def _any_spec():
    return pl.BlockSpec(memory_space=pl.ANY)


def _chunk_rows(c, length):
    if isinstance(c, int):
        return pl.ds(c * length, length)
    return pl.ds(pl.multiple_of(c * length, length), length)


def _mm_kernel(*refs, has_res, split_tile):
    n_x = 1 if split_tile is None else 2
    x_refs, refs = refs[:n_x], refs[n_x:]
    if has_res:
        w_ref, res_ref, gate_ref, o_ref, wbf_ref = refs
    else:
        w_ref, o_ref, wbf_ref = refs
    i = pl.program_id(1)

    @pl.when(i == 0)
    def _():
        wbf_ref[...] = w_ref[...].astype(jnp.bfloat16)

    def emit(x_ref):
        acc = jnp.dot(x_ref[...], wbf_ref[...], preferred_element_type=jnp.float32)
        if has_res:
            acc = res_ref[...] + gate_ref[...] * acc
        o_ref[...] = acc.astype(o_ref.dtype)

    if split_tile is None:
        emit(x_refs[0])
    else:
        pl.when(i < split_tile)(lambda: emit(x_refs[0]))
        pl.when(i >= split_tile)(lambda: emit(x_refs[1]))


def _matmul(x, w, *, layer=None, col0=0, n=None, tm=1024, tn=512, out_dtype=jnp.bfloat16,
            res=None, mods=None, gate_chunk=None, single_buffer_w=False):
    xs = x if isinstance(x, (tuple, list)) else (x,)
    m, k = sum(a.shape[0] for a in xs), xs[0].shape[1]
    if w.ndim == 2:
        w = w[None]
        layer = 0
    n = w.shape[2] - col0 if n is None else n
    tm = min(tm, m)
    assert all(a.shape[0] % tm == 0 for a in xs) and col0 % tn == 0 and w.shape[1] == k
    cb0 = col0 // tn
    grid = (pl.cdiv(n, tn), m // tm)
    w_mode = {"pipeline_mode": pl.Buffered(1)} if single_buffer_w else {}
    if len(xs) == 1:
        split_tile = None
        in_specs = [pl.BlockSpec((tm, k), lambda j, i: (i, 0))]
    else:
        split_tile = xs[0].shape[0] // tm
        in_specs = [pl.BlockSpec((tm, k), lambda j, i: (jnp.minimum(i, split_tile - 1), 0)),
                    pl.BlockSpec((tm, k), lambda j, i: (jnp.maximum(i - split_tile, 0), 0))]
    in_specs.append(pl.BlockSpec((None, k, tn), lambda j, i: (layer, 0, cb0 + j), **w_mode))
    args = [*xs, w]
    has_res = res is not None
    if has_res:
        in_specs += [
            pl.BlockSpec((tm, tn), lambda j, i: (i, j)),
            pl.BlockSpec((None, None, 1, tn), lambda j, i: (_row_group(i, tm), gate_chunk, 0, j)),
        ]
        args += [res, mods]
    n_out = grid[0] * tn
    return pl.pallas_call(
        functools.partial(_mm_kernel, has_res=has_res, split_tile=split_tile),
        name=f"mm_k{k}_n{n}",
        grid=grid,
        in_specs=in_specs,
        out_specs=pl.BlockSpec((tm, tn), lambda j, i: (i, j)),
        out_shape=jax.ShapeDtypeStruct((m, n_out), out_dtype),
        scratch_shapes=[pltpu.VMEM((k, tn), jnp.bfloat16)],
        compiler_params=_params("parallel", "arbitrary"),
    )(*args)


HI16 = 0xFFFF0000


def _pack_rows(y):
    u = lax.bitcast_convert_type(y.astype(jnp.bfloat16).astype(jnp.float32), jnp.uint32)
    half = y.shape[1] // 2
    return (u[:, half:] & jnp.uint32(HI16)) | (u[:, :half] >> 16)


def _unpack_rows(u):
    lo = lax.bitcast_convert_type(u << 16, jnp.float32)
    hi = lax.bitcast_convert_type(u & jnp.uint32(HI16), jnp.float32)
    return jnp.concatenate([lo, hi], axis=1)


def _norm_kernel(*refs, has_mod, has_router):
    it = iter(refs)
    x_ref, w_ref = next(it), next(it)
    shift_ref = scale_ref = wr_ref = None
    if has_mod:
        shift_ref, scale_ref = next(it), next(it)
    if has_router:
        wr_ref = next(it)
    o_ref = next(it)
    x = x_ref[...].astype(jnp.float32)
    y = x * lax.rsqrt(jnp.mean(x * x, axis=-1, keepdims=True) + EPS) * w_ref[...]
    if has_mod:
        y = y * (1.0 + scale_ref[...]) + shift_ref[...]
    o_ref[...] = _pack_rows(y) if o_ref.dtype == jnp.uint32 else y.astype(o_ref.dtype)
    if has_router:
        logits_ref = next(it)
        logits_ref[...] = jnp.dot(y, wr_ref[...], preferred_element_type=jnp.float32,
                                  precision=lax.Precision.HIGHEST)


def _norm(x, w, *, mods=None, shift_chunk=None, scale_chunk=None, router_w=None, tm=256,
          out_dtype=jnp.bfloat16):
    m, d = x.shape
    has_mod = mods is not None
    has_router = router_w is not None
    in_specs = [pl.BlockSpec((tm, d), lambda i: (i, 0)), pl.BlockSpec((1, d), lambda i: (0, 0))]
    args = [x, w.reshape(1, d)]
    if has_mod:
        in_specs += [
            pl.BlockSpec((None, None, 1, d), lambda i: (_row_group(i, tm), shift_chunk, 0, 0)),
            pl.BlockSpec((None, None, 1, d), lambda i: (_row_group(i, tm), scale_chunk, 0, 0)),
        ]
        args += [mods, mods]
    d_out = d // 2 if out_dtype == jnp.uint32 else d
    out_specs = pl.BlockSpec((tm, d_out), lambda i: (i, 0))
    out_shape = jax.ShapeDtypeStruct((m, d_out), out_dtype)
    if has_router:
        in_specs.append(pl.BlockSpec((d, ROUTER_LANES), lambda i: (0, 0)))
        args.append(router_w)
        out_specs = [out_specs, pl.BlockSpec((tm, ROUTER_LANES), lambda i: (i, 0))]
        out_shape = [out_shape, jax.ShapeDtypeStruct((m, ROUTER_LANES), jnp.float32)]
    return pl.pallas_call(
        functools.partial(_norm_kernel, has_mod=has_mod, has_router=has_router),
        name="norm_router" if has_router else "norm",
        grid=(m // tm,),
        in_specs=in_specs,
        out_specs=out_specs,
        out_shape=out_shape,
        compiler_params=_params("parallel"),
    )(*args)


def _combine_kernel(x_ref, y0_ref, y1_ref, g_ref, gate_ref, o_ref):
    g = g_ref[...]
    y = g[:, 0:1] * _unpack_rows(y0_ref[...]) + g[:, 1:2] * _unpack_rows(y1_ref[...])
    o_ref[...] = x_ref[...] + gate_ref[...] * y


def _moe_combine(x, y_assign, gates, mods, gate_chunk, tm=256):
    m, d = x.shape
    assert MOE_TOP_K == 2
    return pl.pallas_call(
        _combine_kernel,
        name="moe_combine",
        grid=(m // tm,),
        in_specs=[
            pl.BlockSpec((tm, d), lambda i: (i, 0)),
            pl.BlockSpec((tm, d // 2), lambda i: (i, 0)),
            pl.BlockSpec((tm, d // 2), lambda i: (m // tm + i, 0)),
            pl.BlockSpec((tm, MOE_TOP_K), lambda i: (i, 0)),
            pl.BlockSpec((None, None, 1, d), lambda i: (_row_group(i, tm), gate_chunk, 0, 0)),
        ],
        out_specs=pl.BlockSpec((tm, d), lambda i: (i, 0)),
        out_shape=jax.ShapeDtypeStruct((m, d), jnp.float32),
        compiler_params=_params("parallel"),
    )(x, y_assign, y_assign, gates, mods)


def _moe_kernel(be_ref, first_ref, rows_ref, next_ref, tok_ref, asg_ref,
                h_hbm, wg_hbm, wu_hbm, wd_hbm, y_hbm,
                xbuf, obuf, stage_g, stage_u, stage_d, wg_bf, wu_bf, wd_bf, w_sems, g_sems, s_sems, *, layer):
    i = pl.program_id(0)
    n_blocks = pl.num_programs(0)
    slot = i % 2

    def weight_copies(e):
        return (pltpu.make_async_copy(wg_hbm.at[layer, e], stage_g, w_sems.at[0]),
                pltpu.make_async_copy(wu_hbm.at[layer, e], stage_u, w_sems.at[1]),
                pltpu.make_async_copy(wd_hbm.at[layer, e], stage_d, w_sems.at[2]))

    def gather_copy(blk, s, r):
        return pltpu.make_async_copy(h_hbm.at[pl.ds(tok_ref[blk * MOE_BLOCK + r], 1)],
                                     xbuf.at[s, pl.ds(r, 1)], g_sems.at[s])

    def scatter_copy(blk, s, r):
        return pltpu.make_async_copy(obuf.at[s, pl.ds(r, 1)],
                                     y_hbm.at[pl.ds(asg_ref[blk * MOE_BLOCK + r], 1)], s_sems.at[s])

    def for_rows(blk, fn, round_up=False):
        n = rows_ref[blk]

        def body(c, carry):
            for u in range(DMA_UNROLL):
                fn(c * DMA_UNROLL + u)
            return carry

        def tail(r, carry):
            fn(r)
            return carry

        if round_up:
            lax.fori_loop(0, (n + DMA_UNROLL - 1) // DMA_UNROLL, body, 0)
        else:
            lax.fori_loop(0, n // DMA_UNROLL, body, 0)
            lax.fori_loop(n // DMA_UNROLL * DMA_UNROLL, n, tail, 0)

    @pl.when(i == 0)
    def _():
        xbuf[...] = jnp.zeros_like(xbuf)
        for_rows(0, lambda r: gather_copy(0, 0, r).start(), round_up=True)
        for cp in weight_copies(be_ref[0]):
            cp.start(priority=1)

    @pl.when(i + 1 < n_blocks)
    def _():
        for_rows(i + 1, lambda r: gather_copy(i + 1, 1 - slot, r).start(), round_up=True)

    @pl.when(first_ref[i] == 1)
    def _():
        for cp in weight_copies(be_ref[i]):
            cp.wait()
        wg_bf[...] = stage_g[...].astype(jnp.bfloat16)
        wu_bf[...] = stage_u[...].astype(jnp.bfloat16)
        wd_bf[...] = stage_d[...].astype(jnp.bfloat16)

        @pl.when(next_ref[i] >= 0)
        def _():
            for cp in weight_copies(next_ref[i]):
                cp.start(priority=1)

    @pl.when(i >= 2)
    def _():
        for_rows(i - 2, lambda r: scatter_copy(i - 2, slot, r).wait())

    @pl.when(rows_ref[i] > 0)
    def _():
        for_rows(i, lambda r: gather_copy(i, slot, r).wait(), round_up=True)
        x = _unpack_rows(xbuf[slot]).astype(jnp.bfloat16)
        g = jnp.dot(x, wg_bf[...], preferred_element_type=jnp.float32)
        u = jnp.dot(x, wu_bf[...], preferred_element_type=jnp.float32)
        a = (g * jax.nn.sigmoid(g) * u).astype(jnp.bfloat16)
        obuf[slot] = _pack_rows(jnp.dot(a, wd_bf[...], preferred_element_type=jnp.float32))
        for_rows(i, lambda r: scatter_copy(i, slot, r).start())

    @pl.when(i == n_blocks - 1)
    def _():
        @pl.when(i >= 1)
        def _():
            for_rows(i - 1, lambda r: scatter_copy(i - 1, 1 - slot, r).wait())
        for_rows(i, lambda r: scatter_copy(i, slot, r).wait())


def _moe_ffn(h_packed, layout, w_gate, w_up, w_down, layer):
    block_expert, block_first, block_rows, block_next, row_token, row_assign = layout
    t, d2 = h_packed.shape
    d = 2 * d2
    n_blocks = block_expert.shape[0]
    grid_spec = pltpu.PrefetchScalarGridSpec(
        num_scalar_prefetch=6,
        grid=(n_blocks,),
        in_specs=[_any_spec(), _any_spec(), _any_spec(), _any_spec()],
        out_specs=_any_spec(),
        scratch_shapes=[
            pltpu.VMEM((2, MOE_BLOCK, d2), jnp.uint32),
            pltpu.VMEM((2, MOE_BLOCK, d2), jnp.uint32),
            pltpu.VMEM((d, MOE_FF), jnp.float32),
            pltpu.VMEM((d, MOE_FF), jnp.float32),
            pltpu.VMEM((MOE_FF, d), jnp.float32),
            pltpu.VMEM((d, MOE_FF), jnp.bfloat16),
            pltpu.VMEM((d, MOE_FF), jnp.bfloat16),
            pltpu.VMEM((MOE_FF, d), jnp.bfloat16),
            pltpu.SemaphoreType.DMA((3,)),
            pltpu.SemaphoreType.DMA((2,)),
            pltpu.SemaphoreType.DMA((2,)),
        ],
    )
    return pl.pallas_call(
        functools.partial(_moe_kernel, layer=layer),
        name="moe_ffn",
        grid_spec=grid_spec,
        out_shape=jax.ShapeDtypeStruct((t * MOE_TOP_K, d2), jnp.uint32),
        compiler_params=_params("arbitrary"),
    )(block_expert, block_first, block_rows, block_next, row_token, row_assign,
      h_packed, w_gate, w_up, w_down)


def _route(logits):
    t = logits.shape[0]
    g_logits = logits[:, :MOE_GROUPS]
    g_sel = jnp.argmax(g_logits, axis=-1)
    g_prob = jnp.take_along_axis(jax.nn.softmax(g_logits, axis=-1), g_sel[:, None], axis=1)
    e_logits = logits[:, MOE_GROUPS:MOE_GROUPS + N_EXPERTS].reshape(t, MOE_GROUPS, MOE_PER_GROUP)
    e_in_group = jnp.take_along_axis(e_logits, g_sel[:, None, None], axis=1)[:, 0]
    top_val, top_idx = lax.top_k(e_in_group, MOE_TOP_K)
    gates = g_prob * jax.nn.softmax(top_val, axis=-1)
    expert = (g_sel[:, None] * MOE_PER_GROUP + top_idx).astype(jnp.int32)
    return gates, expert


def _moe_layout(expert):
    n_assign = expert.size
    e_flat = expert.reshape(-1)
    onehot = (e_flat[:, None] == jnp.arange(N_EXPERTS, dtype=jnp.int32)[None, :]).astype(jnp.int32)
    csum = jnp.cumsum(onehot, axis=0)
    rank = jnp.take_along_axis(csum, e_flat[:, None], axis=1)[:, 0] - 1
    counts = csum[-1]
    padded = (counts + MOE_BLOCK - 1) // MOE_BLOCK * MOE_BLOCK
    p_ends = jnp.cumsum(padded)
    p_starts = p_ends - padded
    pos = (p_starts[e_flat] + rank).astype(jnp.int32)
    n_blocks = -(-n_assign // MOE_BLOCK) + N_EXPERTS
    n_rows = n_blocks * MOE_BLOCK
    assign = jnp.arange(n_assign, dtype=jnp.int32)
    row_assign = jnp.zeros((n_rows,), jnp.int32).at[pos].set(assign)
    row_token = row_assign // MOE_TOP_K
    row_dest = (row_assign % MOE_TOP_K) * (n_assign // MOE_TOP_K) + row_token
    block_start = jnp.arange(n_blocks, dtype=jnp.int32) * MOE_BLOCK
    block_expert = jnp.minimum(jnp.searchsorted(p_ends, block_start, side='right'),
                               N_EXPERTS - 1).astype(jnp.int32)
    valid = block_start < p_ends[-1]
    real_end = (p_starts + counts)[block_expert]
    block_rows = jnp.where(valid, jnp.clip(real_end - block_start, 0, MOE_BLOCK), 0).astype(jnp.int32)
    prev = jnp.concatenate([jnp.full((1,), -1, jnp.int32), block_expert[:-1]])
    block_first = (valid & (block_expert != prev)).astype(jnp.int32)
    after = jnp.minimum(p_ends[block_expert] // MOE_BLOCK, n_blocks - 1)
    has_next = p_ends[block_expert] < p_ends[-1]
    block_next = jnp.where(has_next, block_expert[after], -1).astype(jnp.int32)
    return block_expert, block_first, block_rows, block_next, row_token, row_dest


def _hier_moe(x, h_packed, logits, mods, gate_chunk, w_gate, w_up, w_down, layer):
    gates, expert = _route(logits)
    y_assign = _moe_ffn(h_packed, _moe_layout(expert), w_gate, w_up, w_down, layer)
    return _moe_combine(x, y_assign, gates, mods, gate_chunk)


def _log_sigmoid(x):
    return jnp.minimum(x, 0.0) - jnp.log(1.0 + jnp.exp(-jnp.abs(x)))


def _softplus(x):
    return jnp.maximum(x, 0.0) + jnp.log(1.0 + jnp.exp(-jnp.abs(x)))


def _mlstm_kernel(*refs, seq, has_init, emit_state, n_alias, state_slot):
    f32, bf16 = jnp.float32, jnp.bfloat16
    it = iter(refs)
    q_ref, k_ref, v_ref, o_ref, gc_ref, gr_ref, nw_ref = (next(it) for _ in range(7))
    if has_init:
        c0_ref, n0_ref, m0_ref = next(it), next(it), next(it)
    for _ in range(n_alias):
        next(it)
    y_ref = next(it)
    if emit_state:
        cout_ref, nout_ref, mout_ref = next(it), next(it), next(it)
    hacc, c_sc, n_sc, m_sc = next(it), next(it), next(it), next(it)

    L = MLSTM_CHUNK
    nc = seq // L
    use_inter = has_init or nc > 1
    scale = MLSTM_DQK ** -0.5
    row = lax.broadcasted_iota(jnp.int32, (L, L), 0)
    col = lax.broadcasted_iota(jnp.int32, (L, L), 1)

    for d in range(2):
        tri = (col <= row) if d == 0 else (col >= row)
        tri_t = (row <= col) if d == 0 else (row >= col)
        if has_init:
            c_sc[...] = c0_ref[d]
            n_sc[...] = n0_ref[d]
            m_sc[...] = m0_ref[d]
        else:
            c_sc[...] = jnp.zeros_like(c_sc)
            n_sc[...] = jnp.zeros_like(n_sc)
            m_sc[...] = jnp.zeros_like(m_sc)

        def chunk(ci, carry, d=d, tri=tri, tri_t=tri_t):
            c = ci if d == 0 else nc - 1 - ci
            rows = _chunk_rows(c, L)
            q, k, v = q_ref[rows, :], k_ref[rows, :], v_ref[rows, :]
            gcol, grow = gc_ref[c], gr_ref[c]
            i_col = gcol[:, 2 * d:2 * d + 1]
            f_col = _log_sigmoid(gcol[:, 2 * d + 1:2 * d + 2])
            i_row = grow[2 * d:2 * d + 1, :]
            f_row = _log_sigmoid(grow[2 * d + 1:2 * d + 2, :])
            b_col = jnp.sum(jnp.where(tri, f_row, 0.0), axis=1, keepdims=True)
            b_row = jnp.sum(jnp.where(tri_t, f_col, 0.0), axis=0, keepdims=True)
            b_end = jnp.sum(f_row, axis=1, keepdims=True)
            m_prev = m_sc[...]
            log_d = jnp.where(tri, b_col - b_row + i_row, -jnp.inf)
            log_inter = b_col + m_prev
            m_t = jnp.maximum(log_inter, jnp.max(log_d, axis=1, keepdims=True))
            w_intra = jnp.exp(log_d - m_t)
            qk = lax.dot_general(q, k, (((1,), (1,)), ((), ())), preferred_element_type=f32) * (scale * w_intra)
            num = jnp.dot(qk.astype(bf16), v, preferred_element_type=f32)
            den = jnp.sum(qk, axis=1, keepdims=True)
            if use_inter:
                w_inter = jnp.exp(log_inter - m_t) * scale
                num = num + w_inter * jnp.dot(q, c_sc[...].astype(bf16), preferred_element_type=f32)
                den = den + w_inter * jnp.sum(q.astype(f32) * n_sc[...], axis=1, keepdims=True)
            h = num / jnp.maximum(jnp.abs(den), jnp.exp(-m_t))
            if d == 0:
                hacc[rows, :] = h
            else:
                hacc[rows, :] = hacc[rows, :] + h
            if emit_state or nc > 1:
                log_w = b_end - b_col + i_col
                m_new = jnp.maximum(b_end + m_prev, jnp.max(log_w, axis=0, keepdims=True))
                kw = k.astype(f32) * jnp.exp(log_w - m_new)
                decay = jnp.exp(b_end + m_prev - m_new)
                c_sc[...] = decay * c_sc[...] + jnp.dot(kw.T.astype(bf16), v, preferred_element_type=f32)
                n_sc[...] = decay * n_sc[...] + jnp.sum(kw, axis=0, keepdims=True)
                m_sc[...] = m_new
            return carry

        if nc == 1:
            chunk(0, 0)
        else:
            lax.fori_loop(0, nc, chunk, 0)
        if emit_state and state_slot is None:
            cout_ref[d] = c_sc[...]
            nout_ref[d] = n_sc[...]
            mout_ref[d] = m_sc[...]
        elif emit_state:
            for slot in range(cout_ref.shape[0]):
                if slot == state_slot:
                    cout_ref[slot, d] = c_sc[...]
                    nout_ref[slot, d] = n_sc[...]
                    mout_ref[slot, d] = m_sc[...]
                else:
                    cout_ref[slot, d] = jnp.zeros_like(c_sc)
                    nout_ref[slot, d] = jnp.zeros_like(n_sc)
                    mout_ref[slot, d] = jnp.zeros_like(m_sc)

    hs = hacc[...]
    hn = hs * lax.rsqrt(jnp.mean(hs * hs, axis=-1, keepdims=True) + EPS) * nw_ref[...]
    y_ref[...] = (hn * jax.nn.sigmoid(o_ref[...].astype(f32))).astype(y_ref.dtype)


def _mlstm_seq(proj, gates, norm_w, layer, *, row0, nb, seq, init=None, state_prev=None,
               emit_state=False, n_layers=1):
    f32 = jnp.float32
    H, L = MLSTM_HEADS, MLSTM_CHUNK
    nc = seq // L
    rb0 = row0 // seq
    g = gates[row0:row0 + nb * seq].reshape(nb, nc, L, 2, 2, H)
    g_col = g.transpose(0, 5, 1, 2, 3, 4).reshape(nb, H, nc, L, 4)
    g_row = g.transpose(0, 5, 1, 3, 4, 2).reshape(nb, H, nc, 4, L)
    qb, vb = MLSTM_DQK, MLSTM_DV
    in_specs = [
        pl.BlockSpec((seq, qb), lambda b, h: (rb0 + b, h)),
        pl.BlockSpec((seq, qb), lambda b, h: (rb0 + b, H + h)),
        pl.BlockSpec((seq, vb), lambda b, h: (rb0 + b, H + h)),
        pl.BlockSpec((seq, vb), lambda b, h: (rb0 + b, 2 * H + h)),
        pl.BlockSpec((None, None, nc, L, 4), lambda b, h: (b, h, 0, 0, 0)),
        pl.BlockSpec((None, None, nc, 4, L), lambda b, h: (b, h, 0, 0, 0)),
        pl.BlockSpec((None, None, 1, vb), lambda b, h: (layer, h, 0, 0)),
    ]
    args = [proj, proj, proj, proj, g_col, g_row, norm_w.reshape(-1, H, 1, vb)]
    has_init = init is not None
    if has_init:
        c0, n0, m0 = init
        in_specs += [
            pl.BlockSpec((None, None, 2, None, qb, vb), lambda b, h: (b, layer, 0, h, 0, 0)),
            pl.BlockSpec((None, None, 2, None, 1, qb), lambda b, h: (b, layer, 0, h, 0, 0)),
            pl.BlockSpec((None, None, 2, None, 1, 1), lambda b, h: (b, layer, 0, h, 0, 0)),
        ]
        args += [c0, n0.reshape(*n0.shape[:4], 1, qb), m0.reshape(*m0.shape[:4], 1, 1)]
    out_specs = [pl.BlockSpec((seq, vb), lambda b, h: (b, h))]
    out_shape = [jax.ShapeDtypeStruct((nb * seq, H * vb), jnp.bfloat16)]
    state_slot = None
    if emit_state and state_prev is None:
        state_slot = layer
        out_specs += [
            pl.BlockSpec((None, n_layers, 2, None, qb, vb), lambda b, h: (b, 0, 0, h, 0, 0)),
            pl.BlockSpec((None, n_layers, 2, None, 1, qb), lambda b, h: (b, 0, 0, h, 0, 0)),
            pl.BlockSpec((None, n_layers, 2, None, 1, 1), lambda b, h: (b, 0, 0, h, 0, 0)),
        ]
    elif emit_state:
        out_specs += [
            pl.BlockSpec((None, None, 2, None, qb, vb), lambda b, h: (b, layer, 0, h, 0, 0)),
            pl.BlockSpec((None, None, 2, None, 1, qb), lambda b, h: (b, layer, 0, h, 0, 0)),
            pl.BlockSpec((None, None, 2, None, 1, 1), lambda b, h: (b, layer, 0, h, 0, 0)),
        ]
    if emit_state:
        out_shape += [
            jax.ShapeDtypeStruct((nb, n_layers, 2, H, qb, vb), f32),
            jax.ShapeDtypeStruct((nb, n_layers, 2, H, 1, qb), f32),
            jax.ShapeDtypeStruct((nb, n_layers, 2, H, 1, 1), f32),
        ]
    aliases = {}
    n_alias = 0
    if state_prev is not None:
        for k_out, a in enumerate(state_prev):
            aliases[len(args)] = 1 + k_out
            in_specs.append(_any_spec())
            args.append(a)
            n_alias += 1
    return pl.pallas_call(
        functools.partial(_mlstm_kernel, seq=seq, has_init=has_init, emit_state=emit_state, n_alias=n_alias,
                          state_slot=state_slot),
        name=f"mlstm_s{seq}",
        grid=(nb, H),
        in_specs=in_specs,
        out_specs=out_specs,
        out_shape=out_shape,
        input_output_aliases=aliases,
        scratch_shapes=[
            pltpu.VMEM((seq, vb), f32),
            pltpu.VMEM((qb, vb), f32),
            pltpu.VMEM((1, qb), f32),
            pltpu.VMEM((1, 1), f32),
        ],
        compiler_params=_params("arbitrary", "arbitrary"),
    )(*args)


def _conv_silu(x, w_ref, b_ref):
    s = x.shape[0]
    t = lax.broadcasted_iota(jnp.int32, x.shape, 0)
    y = (w_ref[0:1, :] * jnp.where(t >= 2, pltpu.roll(x, 2, 0), 0.0)
         + w_ref[1:2, :] * jnp.where(t >= 1, pltpu.roll(x, 1, 0), 0.0)
         + w_ref[2:3, :] * x
         + w_ref[3:4, :] * jnp.where(t < s - 1, pltpu.roll(x, s - 1, 0), 0.0)
         + b_ref[...])
    return y * jax.nn.sigmoid(y)


def _ssd_kernel(*refs, seq, has_init, emit_state):
    f32, bf16 = jnp.float32, jnp.bfloat16
    it = iter(refs)
    (x_ref, b_ref, c_ref, z_ref, dtc_ref, dtr_ref, dtbc_ref, dtbr_ref, alc_ref, alr_ref,
     cwx_ref, cbx_ref, cwb_ref, cbb_ref, cwc_ref, cbc_ref, dsk_ref, nw_ref) = (next(it) for _ in range(18))
    if has_init:
        h0_ref = next(it)
    y_ref = next(it)
    if emit_state:
        hout_ref = next(it)
    xc_sc, bc_sc, cc_sc, yacc, hst = (next(it) for _ in range(5))

    L = SSD_CHUNK
    nc = seq // L
    E, P = SSD_GROUP_HEADS, SSD_HEAD_DIM
    n_pair = E // 2
    W = 2 * P

    for p in range(n_pair):
        cols = slice(p * W, (p + 1) * W)
        xc_sc[:, cols] = _conv_silu(x_ref[:, cols].astype(f32), cwx_ref.at[:, cols], cbx_ref.at[:, cols])
    bc_sc[...] = _conv_silu(b_ref[...].astype(f32), cwb_ref, cbb_ref).astype(bf16)
    cc_sc[...] = _conv_silu(c_ref[...].astype(f32), cwc_ref, cbc_ref).astype(bf16)

    row = lax.broadcasted_iota(jnp.int32, (L, L), 0)
    col = lax.broadcasted_iota(jnp.int32, (L, L), 1)
    lo = lax.broadcasted_iota(jnp.int32, (L, W), 1) < P
    lo_rows = lax.broadcasted_iota(jnp.int32, (W, SSD_STATE), 0) < P
    a_col = -jnp.exp(alc_ref[...])
    a_row = -jnp.exp(alr_ref[...])

    for d in range(2):
        tri = (col <= row) if d == 0 else (col >= row)
        tri_f = tri.astype(f32)
        tri_tf = ((row <= col) if d == 0 else (row >= col)).astype(f32)
        if has_init:
            hst[...] = h0_ref[d]
        else:
            hst[...] = jnp.zeros_like(hst)

        def chunk(ci, carry, d=d, tri=tri, tri_f=tri_f, tri_tf=tri_tf):
            c = ci if d == 0 else nc - 1 - ci
            rows = _chunk_rows(c, L)
            dt_c = _softplus(dtc_ref[c] + dtbc_ref[...])
            dt_r = _softplus(dtr_ref[c] + dtbr_ref[...])
            cs_c = jnp.dot(tri_f, dt_c * a_col, preferred_element_type=f32, precision=lax.Precision.HIGHEST)
            cs_r = jnp.dot(dt_r * a_row, tri_tf, preferred_element_type=f32, precision=lax.Precision.HIGHEST)
            cs_e = cs_c[L - 1:L, :] if d == 0 else cs_c[0:1, :]
            bm, cm = bc_sc[rows, :], cc_sc[rows, :]
            cb = lax.dot_general(cm, bm, (((1,), (1,)), ((), ())), preferred_element_type=f32)
            for p in range(n_pair):
                cols = slice(p * W, (p + 1) * W)
                ia, ib = d * E + 2 * p, d * E + 2 * p + 1
                xdt = xc_sc[rows, cols] * jnp.where(lo, dt_c[:, ia:ia + 1], dt_c[:, ib:ib + 1])
                xdt_bf = xdt.astype(bf16)
                cs_ta, cs_tb = cs_c[:, ia:ia + 1], cs_c[:, ib:ib + 1]
                wa = (cb * jnp.exp(jnp.where(tri, cs_ta - cs_r[ia:ia + 1, :], -jnp.inf))).astype(bf16)
                wb = (cb * jnp.exp(jnp.where(tri, cs_tb - cs_r[ib:ib + 1, :], -jnp.inf))).astype(bf16)
                y = jnp.where(lo, jnp.dot(wa, xdt_bf, preferred_element_type=f32),
                              jnp.dot(wb, xdt_bf, preferred_element_type=f32))
                hrows = slice(p * W, (p + 1) * W)
                hp = hst[hrows, :]
                if has_init or nc > 1:
                    y_inter = lax.dot_general(cm, hp.astype(bf16), (((1,), (1,)), ((), ())),
                                              preferred_element_type=f32)
                    y = y + jnp.where(lo, jnp.exp(cs_ta), jnp.exp(cs_tb)) * y_inter
                if d == 0:
                    yacc[rows, cols] = y
                else:
                    yacc[rows, cols] = yacc[rows, cols] + y
                if emit_state or nc > 1:
                    ea, eb = cs_e[:, ia:ia + 1], cs_e[:, ib:ib + 1]
                    xw = xdt * jnp.where(lo, jnp.exp(ea - cs_ta), jnp.exp(eb - cs_tb))
                    contrib = jnp.dot(xw.T.astype(bf16), bm, preferred_element_type=f32)
                    hst[hrows, :] = jnp.where(lo_rows, jnp.exp(ea), jnp.exp(eb)) * hp + contrib
            return carry

        if nc == 1:
            chunk(0, 0)
        else:
            lax.fori_loop(0, nc, chunk, 0)
        if emit_state:
            hout_ref[d] = hst[...]

    y = yacc[...] + dsk_ref[...] * xc_sc[...]
    z = z_ref[...].astype(f32)
    y = y * (z * jax.nn.sigmoid(z))
    y = y * lax.rsqrt(jnp.mean(y * y, axis=-1, keepdims=True) + EPS) * nw_ref[...]
    y_ref[...] = y.astype(y_ref.dtype)


def _ssd_seq(zx, dt_raw, conv_w, conv_b, dt_bias, a_log, d_skip, norm_w, layer, *, row0, nb, seq,
             init=None, emit_state=False):
    f32 = jnp.float32
    G, E, L, N = SSD_GROUPS, SSD_GROUP_HEADS, SSD_CHUNK, SSD_STATE
    gc = SSD_GROUP_COLS
    nc = seq // L
    rb0 = row0 // seq
    dt = dt_raw[row0:row0 + nb * seq].reshape(nb, nc, L, 2, G, E)
    dt_col = dt.transpose(0, 4, 1, 2, 3, 5).reshape(nb, G, nc, L, 2 * E)
    dt_row = dt.transpose(0, 4, 1, 3, 5, 2).reshape(nb, G, nc, 2 * E, L)

    def per_group(a):
        a = a.astype(f32).reshape(2, G, E).transpose(1, 0, 2).reshape(G, 2 * E)
        return a[:, None, :], a[:, :, None]

    dtb_c, dtb_r = per_group(dt_bias)
    al_c, al_r = per_group(a_log)
    x_cb = SSD_INNER // gc
    bc0 = 2 * SSD_INNER // N
    cw = conv_w.astype(f32)
    cbias = conv_b.astype(f32).reshape(1, -1)
    dsk = jnp.repeat(d_skip.astype(f32), SSD_HEAD_DIM).reshape(1, SSD_INNER)
    in_specs = [
        pl.BlockSpec((seq, gc), lambda b, g: (rb0 + b, x_cb + g)),
        pl.BlockSpec((seq, N), lambda b, g: (rb0 + b, bc0 + g)),
        pl.BlockSpec((seq, N), lambda b, g: (rb0 + b, bc0 + G + g)),
        pl.BlockSpec((seq, gc), lambda b, g: (rb0 + b, g)),
        pl.BlockSpec((None, None, nc, L, 2 * E), lambda b, g: (b, g, 0, 0, 0)),
        pl.BlockSpec((None, None, nc, 2 * E, L), lambda b, g: (b, g, 0, 0, 0)),
        pl.BlockSpec((None, 1, 2 * E), lambda b, g: (g, 0, 0)),
        pl.BlockSpec((None, 2 * E, 1), lambda b, g: (g, 0, 0)),
        pl.BlockSpec((None, 1, 2 * E), lambda b, g: (g, 0, 0)),
        pl.BlockSpec((None, 2 * E, 1), lambda b, g: (g, 0, 0)),
        pl.BlockSpec((SSD_CONV, gc), lambda b, g: (0, g)),
        pl.BlockSpec((1, gc), lambda b, g: (0, g)),
        pl.BlockSpec((SSD_CONV, N), lambda b, g: (0, SSD_INNER // N + g)),
        pl.BlockSpec((1, N), lambda b, g: (0, SSD_INNER // N + g)),
        pl.BlockSpec((SSD_CONV, N), lambda b, g: (0, SSD_INNER // N + G + g)),
        pl.BlockSpec((1, N), lambda b, g: (0, SSD_INNER // N + G + g)),
        pl.BlockSpec((1, gc), lambda b, g: (0, g)),
        pl.BlockSpec((1, gc), lambda b, g: (0, g)),
    ]
    args = [zx, zx, zx, zx, dt_col, dt_row, dtb_c, dtb_r, al_c, al_r,
            cw, cbias, cw, cbias, cw, cbias, dsk, norm_w.astype(f32).reshape(1, SSD_INNER)]
    has_init = init is not None
    state_block = pl.BlockSpec((None, None, 2, None, E * SSD_HEAD_DIM, N), lambda b, g: (b, layer, 0, g, 0, 0))
    if has_init:
        in_specs.append(state_block)
        args.append(init.reshape(*init.shape[:3], G, E * SSD_HEAD_DIM, N))
    out_specs = [pl.BlockSpec((seq, gc), lambda b, g: (b, g))]
    out_shape = [jax.ShapeDtypeStruct((nb * seq, SSD_INNER), jnp.bfloat16)]
    if emit_state:
        out_specs.append(state_block)
        out_shape.append(jax.ShapeDtypeStruct((nb, 1, 2, G, E * SSD_HEAD_DIM, N), f32))
    return pl.pallas_call(
        functools.partial(_ssd_kernel, seq=seq, has_init=has_init, emit_state=emit_state),
        name=f"ssd_s{seq}",
        grid=(nb, G),
        in_specs=in_specs,
        out_specs=out_specs,
        out_shape=out_shape,
        scratch_shapes=[
            pltpu.VMEM((seq, gc), f32),
            pltpu.VMEM((seq, N), jnp.bfloat16),
            pltpu.VMEM((seq, N), jnp.bfloat16),
            pltpu.VMEM((seq, gc), f32),
            pltpu.VMEM((E * SSD_HEAD_DIM, N), f32),
        ],
        compiler_params=_params("arbitrary", "arbitrary"),
    )(*args)


def _attn_kernel(*refs, heads, rope):
    f32, bf16 = jnp.float32, jnp.bfloat16
    if rope:
        qn_ref, qp_ref, kv_ref, kpe_ref, cos_ref, sin_ref, o_ref = refs
    else:
        qn_ref, qp_ref, kv_ref, kpe_ref, o_ref = refs
    scale = (MLA_NOPE + MLA_ROPE) ** -0.5
    kpe2 = kpe_ref[...]
    lq = qn_ref.shape[0]
    lane = lax.broadcasted_iota(jnp.int32, (lq, V7X_LANES), 1)
    nt = (((1,), (1,)), ((), ()))
    for pr in range(heads // 2):
        qp = qp_ref[:, pr * V7X_LANES:(pr + 1) * V7X_LANES]
        if rope:
            x = qp.astype(f32)
            other = jnp.where(lane % (2 * ROPE_FREQS) < ROPE_FREQS,
                              pltpu.roll(x, V7X_LANES - ROPE_FREQS, 1), pltpu.roll(x, ROPE_FREQS, 1))
            qp = (x * cos_ref[...] + other * sin_ref[...]).astype(bf16)
        for a in range(2):
            h = 2 * pr + a
            qpa = jnp.where((lane < MLA_ROPE) if a == 0 else (lane >= MLA_ROPE), qp, jnp.zeros_like(qp))
            kn = kv_ref[:, h * 2 * MLA_NOPE:h * 2 * MLA_NOPE + MLA_NOPE]
            vv = kv_ref[:, h * 2 * MLA_NOPE + MLA_NOPE:(h + 1) * 2 * MLA_NOPE]
            s = (lax.dot_general(qn_ref[:, h * MLA_NOPE:(h + 1) * MLA_NOPE], kn, nt, preferred_element_type=f32)
                 + lax.dot_general(qpa, kpe2, nt, preferred_element_type=f32)) * scale
            e = jnp.exp(s - jnp.max(s, axis=-1, keepdims=True))
            p = e / jnp.sum(e, axis=-1, keepdims=True)
            o_ref[:, h * MLA_V:(h + 1) * MLA_V] = jnp.dot(p.astype(bf16), vv,
                                                          preferred_element_type=f32).astype(o_ref.dtype)


def _attention(qn, qp, kv, kpe2, *, row0, nb, seq, key0, n_keys, q_tile, heads, tables=None):
    nq = seq // q_tile
    qb0 = row0 // q_tile
    kb0 = key0 // n_keys
    assert row0 % q_tile == 0 and key0 % n_keys == 0
    hb = MLA_HEADS // heads
    in_specs = [
        pl.BlockSpec((q_tile, heads * MLA_NOPE), lambda b, t, h: (qb0 + b * nq + t, h)),
        pl.BlockSpec((q_tile, heads * MLA_ROPE), lambda b, t, h: (qb0 + b * nq + t, h)),
        pl.BlockSpec((n_keys, heads * 2 * MLA_NOPE), lambda b, t, h: (kb0 + b, h)),
        pl.BlockSpec((n_keys, V7X_LANES), lambda b, t, h: (kb0 + b, 0)),
    ]
    args = [qn, qp, kv, kpe2]
    rope = tables is not None
    if rope:
        in_specs += [pl.BlockSpec((q_tile, V7X_LANES), lambda b, t, h: (t, 0))] * 2
        args += list(tables)
    return pl.pallas_call(
        functools.partial(_attn_kernel, heads=heads, rope=rope),
        name=f"attn_s{seq}",
        grid=(nb, nq, hb),
        in_specs=in_specs,
        out_specs=pl.BlockSpec((q_tile, heads * MLA_V), lambda b, t, h: (b * nq + t, h)),
        out_shape=jax.ShapeDtypeStruct((nb * seq, MLA_HEADS * MLA_V), jnp.bfloat16),
        compiler_params=_params("arbitrary", "arbitrary", "arbitrary"),
    )(*args)


def _rope_tables(n_tok):
    rows = n_tok // GRID_W
    row = jnp.repeat(jnp.arange(rows, dtype=jnp.float32), GRID_W)
    col = jnp.tile(jnp.arange(GRID_W, dtype=jnp.float32), rows)
    inv_freq = ROPE_THETA ** (-jnp.arange(ROPE_FREQS, dtype=jnp.float32) / ROPE_FREQS)
    ang = jnp.stack([row, col], axis=-1)[:, :, None] * inv_freq
    cos, sin = jnp.cos(ang), jnp.sin(ang)
    cos64 = jnp.concatenate([cos[:, 0], cos[:, 0], cos[:, 1], cos[:, 1]], axis=-1)
    sin64 = jnp.concatenate([-sin[:, 0], sin[:, 0], -sin[:, 1], sin[:, 1]], axis=-1)
    return jnp.tile(cos64, (1, 2)), jnp.tile(sin64, (1, 2))


def _rope_rows(x, cos64, sin64):
    xr = x.reshape(x.shape[0], 2, 2, ROPE_FREQS)
    other = jnp.stack([xr[:, :, 1], xr[:, :, 0]], axis=2).reshape(x.shape)
    return x * cos64 + other * sin64


def kernel(x_prompt, x_sample, state_mlstm_C, state_mlstm_n, state_mlstm_m, state_ssm, cache_mla_ckv, cache_mla_kpe, c, c_ctx, ada_w, ada_b, norm_mix, norm_ffn, mlstm_w_in, mlstm_b_gates, mlstm_norm, mlstm_w_out, ssd_w_in, ssd_conv_w, ssd_conv_b, ssd_dt_bias, ssd_A_log, ssd_D, ssd_norm, ssd_w_out, mla_w_in, mla_q_norm, mla_kv_norm, mla_w_qb, mla_w_kvb, mla_w_out, router_group_w, router_expert_w, moe_w_gate, moe_w_up, moe_w_down, final_norm):
    f32, bf16 = jnp.float32, jnp.bfloat16
    D = D_MODEL
    X = jnp.concatenate([x_prompt.reshape(N_PROMPT, D), x_sample.reshape(N_SAMPLE, D)], axis=0)

    cond = jnp.concatenate([c_ctx[None, :], c], axis=0)
    cond = jnp.pad(jax.nn.silu(cond), ((0, 16 - cond.shape[0]), (0, 0))).astype(bf16)

    n_mlstm = mlstm_w_in.shape[0]
    mlstm_state = None
    new_ssm, new_ckv, new_kpe = [], [], []
    for l in range(DEPTH):
        kind, j = l % N_MIXERS, l // N_MIXERS
        ada = _matmul(cond, ada_w, layer=l, tm=16, tn=512, out_dtype=f32)
        mods = (ada[:1 + DEC_BATCH] + ada_b[l]).reshape(1 + DEC_BATCH, N_MOD, 1, D)
        h = _norm(X, norm_mix[l], mods=mods, shift_chunk=0, scale_chunk=1)

        if kind == 0:
            proj = _matmul(h, mlstm_w_in, layer=j, n=12288)
            n_gate = 4 * MLSTM_HEADS
            gates = _matmul(h, mlstm_w_in[j, :, 12288:], tn=n_gate, out_dtype=f32) + mlstm_b_gates[j]
            outs = _mlstm_seq(proj, gates, mlstm_norm, j, row0=0, nb=BATCH, seq=SEQ, emit_state=True,
                              state_prev=mlstm_state, n_layers=n_mlstm)
            y_p, mlstm_state = outs[0], tuple(outs[1:])
            y_s = _mlstm_seq(proj, gates, mlstm_norm, j, row0=N_PROMPT, nb=DEC_BATCH, seq=DEC_SEQ,
                             init=(state_mlstm_C, state_mlstm_n, state_mlstm_m))[0]
            X = _matmul((y_p, y_s), mlstm_w_out, layer=j, tm=512, out_dtype=f32, res=X, mods=mods, gate_chunk=2)
        elif kind == 1:
            zx = _matmul(h, ssd_w_in, layer=j, n=18432)
            dt_raw = _matmul(h, ssd_w_in, layer=j, col0=18432, n=256, tn=256, out_dtype=f32)
            w = (ssd_conv_w[j], ssd_conv_b[j], ssd_dt_bias[j], ssd_A_log[j], ssd_D[j], ssd_norm[j])
            y_p, s_h = _ssd_seq(zx, dt_raw, *w, 0, row0=0, nb=BATCH, seq=SEQ, emit_state=True)
            y_s = _ssd_seq(zx, dt_raw, *w, j, row0=N_PROMPT, nb=DEC_BATCH, seq=DEC_SEQ, init=state_ssm)[0]
            new_ssm.append(s_h.reshape(BATCH, 2, SSD_HEADS, SSD_HEAD_DIM, SSD_STATE))
            X = _matmul(jnp.concatenate([y_p, y_s], axis=0), ssd_w_out, layer=j, tm=512, tn=512, out_dtype=f32, res=X, mods=mods, gate_chunk=2,
                        single_buffer_w=True)
        else:
            lat = _matmul(h, mla_w_in, layer=j, n=2048, out_dtype=f32)
            kpe = _matmul(h, mla_w_in[j, :, 2048:], tn=MLA_ROPE, out_dtype=f32)
            hq = _norm(lat[:, :MLA_Q_RANK], mla_q_norm[j])
            ckv = _norm(lat[:, MLA_Q_RANK:], mla_kv_norm[j], out_dtype=f32)
            wq = mla_w_qb[j].reshape(MLA_Q_RANK, MLA_HEADS, MLA_NOPE + MLA_ROPE)
            qn = _matmul(hq, wq[:, :, :MLA_NOPE].reshape(MLA_Q_RANK, -1), tn=512)
            qp = _matmul(hq, wq[:, :, MLA_NOPE:].reshape(MLA_Q_RANK, -1), tn=512)
            n_keys = PAST_LEN + DEC_SEQ
            n_skeys = DEC_BATCH * n_keys
            cos2, sin2 = _rope_tables(DEC_SEQ)
            ckv_s = jnp.concatenate([cache_mla_ckv[:, j], ckv[N_PROMPT:].reshape(DEC_BATCH, DEC_SEQ, -1)], axis=1)
            ckv_all = jnp.concatenate([ckv_s.reshape(-1, MLA_KV_RANK), ckv[:N_PROMPT]], axis=0).astype(bf16)
            kv = _matmul(ckv_all, mla_w_kvb, layer=j, tn=512)
            kpe_lat = _rope_rows(kpe[N_PROMPT:], jnp.tile(cos2[:, :MLA_ROPE], (DEC_BATCH, 1)),
                                 jnp.tile(sin2[:, :MLA_ROPE], (DEC_BATCH, 1)))
            kpe_s = jnp.concatenate([cache_mla_kpe[:, j], kpe_lat.reshape(DEC_BATCH, DEC_SEQ, -1)], axis=1)
            kpe_all = jnp.concatenate([kpe_s.reshape(-1, MLA_ROPE), kpe[:N_PROMPT]], axis=0).astype(bf16)
            kpe2 = jnp.concatenate([kpe_all, kpe_all], axis=1)
            o_p = _attention(qn, qp, kv, kpe2, row0=0, nb=BATCH, seq=SEQ, key0=n_skeys, n_keys=SEQ, q_tile=SEQ,
                             heads=8)
            o_s = _attention(qn, qp, kv, kpe2, row0=N_PROMPT, nb=DEC_BATCH, seq=DEC_SEQ, key0=0,
                             n_keys=n_keys, q_tile=ATTN_Q_TILE, heads=2, tables=(cos2, sin2))
            new_ckv.append(ckv[:N_PROMPT].reshape(BATCH, SEQ, MLA_KV_RANK))
            new_kpe.append(kpe[:N_PROMPT].reshape(BATCH, SEQ, MLA_ROPE))
            X = _matmul(jnp.concatenate([o_p, o_s], axis=0), mla_w_out, layer=j, tm=512, tn=512, out_dtype=f32, res=X, mods=mods, gate_chunk=2,
                        single_buffer_w=True)

        router_w = jnp.pad(jnp.concatenate([router_group_w[l], router_expert_w[l]], axis=1),
                           ((0, 0), (0, ROUTER_LANES - MOE_GROUPS - N_EXPERTS)))
        hf, logits = _norm(X, norm_ffn[l], mods=mods, shift_chunk=3, scale_chunk=4, router_w=router_w,
                           out_dtype=jnp.uint32)
        X = _hier_moe(X, hf, logits, mods, 5, moe_w_gate, moe_w_up, moe_w_down, l)

    Y = _norm(X, final_norm, out_dtype=f32)
    y_prompt = Y[:N_PROMPT].reshape(BATCH, SEQ, D)
    y_sample = Y[N_PROMPT:].reshape(DEC_BATCH, DEC_SEQ, D)
    new_c, new_n, new_m = mlstm_state
    return (y_prompt, y_sample, new_c,
            new_n.reshape(BATCH, n_mlstm, 2, MLSTM_HEADS, MLSTM_DQK),
            new_m.reshape(BATCH, n_mlstm, 2, MLSTM_HEADS),
            jnp.stack(new_ssm, axis=1), jnp.stack(new_ckv, axis=1), jnp.stack(new_kpe, axis=1))
```

```python
import functools

import jax
import jax.numpy as jnp
from jax import lax
from jax.experimental import pallas as pl
from jax.experimental.pallas import tpu as pltpu

D_MODEL = 4096
BATCH = 32
SEQ = 256
DEPTH = 4
DEC_BATCH = 2
DEC_SEQ = 1024
PAST_LEN = 512
GRID_W = 64
N_MIXERS = 3
EPS = 1e-6

MLSTM_HEADS = 8
MLSTM_DQK = 256
MLSTM_DV = 512

SSD_INNER = 8192
SSD_HEAD_DIM = 64
SSD_HEADS = 128
SSD_GROUPS = 8
SSD_STATE = 128
SSD_CONV = 4
SSD_GROUP_HEADS = SSD_HEADS // SSD_GROUPS
SSD_GROUP_COLS = SSD_INNER // SSD_GROUPS

MLA_HEADS = 64
MLA_Q_RANK = 1536
MLA_KV_RANK = 512
MLA_NOPE = 128
MLA_ROPE = 64
MLA_V = 128
ROPE_THETA = 10000.0
ROPE_FREQS = 16

MOE_GROUPS = 8
MOE_PER_GROUP = 8
N_EXPERTS = 64
MOE_TOP_K = 2
MOE_FF = 512

N_PROMPT = BATCH * SEQ
N_SAMPLE = DEC_BATCH * DEC_SEQ
N_TOK = N_PROMPT + N_SAMPLE
N_MOD = 6

V7X_VMEM_LIMIT = 56 * 1024 * 1024
V7X_LANES = 128
MOE_BLOCK = 256
DMA_UNROLL = 8
ROUTER_LANES = V7X_LANES
MLSTM_CHUNK = 256
SSD_CHUNK = 128
ATTN_Q_TILE = 512


def _params(*sem):
    return pltpu.CompilerParams(dimension_semantics=sem, vmem_limit_bytes=V7X_VMEM_LIMIT)


def _row_group(tile, rows_per_tile):
    first_sample = N_PROMPT // rows_per_tile
    per_batch = DEC_SEQ // rows_per_tile
    return jnp.where(tile < first_sample, 0, 1 + (tile - first_sample) // per_batch)


def _any_spec():
    return pl.BlockSpec(memory_space=pl.ANY)


def _chunk_rows(c, length):
    if isinstance(c, int):
        return pl.ds(c * length, length)
    return pl.ds(pl.multiple_of(c * length, length), length)


def _mm_kernel(*refs, has_res, split_tile):
    n_x = 1 if split_tile is None else 2
    x_refs, refs = refs[:n_x], refs[n_x:]
    if has_res:
        w_ref, res_ref, gate_ref, o_ref, wbf_ref = refs
    else:
        w_ref, o_ref, wbf_ref = refs
    i = pl.program_id(1)

    @pl.when(i == 0)
    def _():
        wbf_ref[...] = w_ref[...].astype(jnp.bfloat16)

    def emit(x_ref):
        acc = jnp.dot(x_ref[...], wbf_ref[...], preferred_element_type=jnp.float32)
        if has_res:
            acc = res_ref[...] + gate_ref[...] * acc
        o_ref[...] = acc.astype(o_ref.dtype)

    if split_tile is None:
        emit(x_refs[0])
    else:
        pl.when(i < split_tile)(lambda: emit(x_refs[0]))
        pl.when(i >= split_tile)(lambda: emit(x_refs[1]))


def _matmul(x, w, *, layer=None, col0=0, n=None, tm=1024, tn=512, out_dtype=jnp.bfloat16,
            res=None, mods=None, gate_chunk=None, single_buffer_w=False):
    xs = x if isinstance(x, (tuple, list)) else (x,)
    m, k = sum(a.shape[0] for a in xs), xs[0].shape[1]
    if w.ndim == 2:
        w = w[None]
        layer = 0
    n = w.shape[2] - col0 if n is None else n
    tm = min(tm, m)
    assert all(a.shape[0] % tm == 0 for a in xs) and col0 % tn == 0 and w.shape[1] == k
    cb0 = col0 // tn
    grid = (pl.cdiv(n, tn), m // tm)
    w_mode = {"pipeline_mode": pl.Buffered(1)} if single_buffer_w else {}
    if len(xs) == 1:
        split_tile = None
        in_specs = [pl.BlockSpec((tm, k), lambda j, i: (i, 0))]
    else:
        split_tile = xs[0].shape[0] // tm
        in_specs = [pl.BlockSpec((tm, k), lambda j, i: (jnp.minimum(i, split_tile - 1), 0)),
                    pl.BlockSpec((tm, k), lambda j, i: (jnp.maximum(i - split_tile, 0), 0))]
    in_specs.append(pl.BlockSpec((None, k, tn), lambda j, i: (layer, 0, cb0 + j), **w_mode))
    args = [*xs, w]
    has_res = res is not None
    if has_res:
        in_specs += [
            pl.BlockSpec((tm, tn), lambda j, i: (i, j)),
            pl.BlockSpec((None, None, 1, tn), lambda j, i: (_row_group(i, tm), gate_chunk, 0, j)),
        ]
        args += [res, mods]
    n_out = grid[0] * tn
    return pl.pallas_call(
        functools.partial(_mm_kernel, has_res=has_res, split_tile=split_tile),
        name=f"mm_k{k}_n{n}",
        grid=grid,
        in_specs=in_specs,
        out_specs=pl.BlockSpec((tm, tn), lambda j, i: (i, j)),
        out_shape=jax.ShapeDtypeStruct((m, n_out), out_dtype),
        scratch_shapes=[pltpu.VMEM((k, tn), jnp.bfloat16)],
        compiler_params=_params("parallel", "arbitrary"),
    )(*args)


HI16 = 0xFFFF0000


def _pack_rows(y):
    u = lax.bitcast_convert_type(y.astype(jnp.bfloat16).astype(jnp.float32), jnp.uint32)
    half = y.shape[1] // 2
    return (u[:, half:] & jnp.uint32(HI16)) | (u[:, :half] >> 16)


def _unpack_rows(u):
    lo = lax.bitcast_convert_type(u << 16, jnp.float32)
    hi = lax.bitcast_convert_type(u & jnp.uint32(HI16), jnp.float32)
    return jnp.concatenate([lo, hi], axis=1)


def _norm_kernel(*refs, has_mod, has_router):
    it = iter(refs)
    x_ref, w_ref = next(it), next(it)
    shift_ref = scale_ref = wr_ref = None
    if has_mod:
        shift_ref, scale_ref = next(it), next(it)
    if has_router:
        wr_ref = next(it)
    o_ref = next(it)
    x = x_ref[...].astype(jnp.float32)
    y = x * lax.rsqrt(jnp.mean(x * x, axis=-1, keepdims=True) + EPS) * w_ref[...]
    if has_mod:
        y = y * (1.0 + scale_ref[...]) + shift_ref[...]
    o_ref[...] = _pack_rows(y) if o_ref.dtype == jnp.uint32 else y.astype(o_ref.dtype)
    if has_router:
        logits_ref = next(it)
        logits_ref[...] = jnp.dot(y, wr_ref[...], preferred_element_type=jnp.float32,
                                  precision=lax.Precision.HIGHEST)


def _norm(x, w, *, mods=None, shift_chunk=None, scale_chunk=None, router_w=None, tm=256,
          out_dtype=jnp.bfloat16):
    m, d = x.shape
    has_mod = mods is not None
    has_router = router_w is not None
    in_specs = [pl.BlockSpec((tm, d), lambda i: (i, 0)), pl.BlockSpec((1, d), lambda i: (0, 0))]
    args = [x, w.reshape(1, d)]
    if has_mod:
        in_specs += [
            pl.BlockSpec((None, None, 1, d), lambda i: (_row_group(i, tm), shift_chunk, 0, 0)),
            pl.BlockSpec((None, None, 1, d), lambda i: (_row_group(i, tm), scale_chunk, 0, 0)),
        ]
        args += [mods, mods]
    d_out = d // 2 if out_dtype == jnp.uint32 else d
    out_specs = pl.BlockSpec((tm, d_out), lambda i: (i, 0))
    out_shape = jax.ShapeDtypeStruct((m, d_out), out_dtype)
    if has_router:
        in_specs.append(pl.BlockSpec((d, ROUTER_LANES), lambda i: (0, 0)))
        args.append(router_w)
        out_specs = [out_specs, pl.BlockSpec((tm, ROUTER_LANES), lambda i: (i, 0))]
        out_shape = [out_shape, jax.ShapeDtypeStruct((m, ROUTER_LANES), jnp.float32)]
    return pl.pallas_call(
        functools.partial(_norm_kernel, has_mod=has_mod, has_router=has_router),
        name="norm_router" if has_router else "norm",
        grid=(m // tm,),
        in_specs=in_specs,
        out_specs=out_specs,
        out_shape=out_shape,
        compiler_params=_params("parallel"),
    )(*args)


def _combine_kernel(x_ref, y0_ref, y1_ref, g_ref, gate_ref, o_ref):
    g = g_ref[...]
    y = g[:, 0:1] * _unpack_rows(y0_ref[...]) + g[:, 1:2] * _unpack_rows(y1_ref[...])
    o_ref[...] = x_ref[...] + gate_ref[...] * y


def _moe_combine(x, y_assign, gates, mods, gate_chunk, tm=256):
    m, d = x.shape
    assert MOE_TOP_K == 2
    return pl.pallas_call(
        _combine_kernel,
        name="moe_combine",
        grid=(m // tm,),
        in_specs=[
            pl.BlockSpec((tm, d), lambda i: (i, 0)),
            pl.BlockSpec((tm, d // 2), lambda i: (i, 0)),
            pl.BlockSpec((tm, d // 2), lambda i: (m // tm + i, 0)),
            pl.BlockSpec((tm, MOE_TOP_K), lambda i: (i, 0)),
            pl.BlockSpec((None, None, 1, d), lambda i: (_row_group(i, tm), gate_chunk, 0, 0)),
        ],
        out_specs=pl.BlockSpec((tm, d), lambda i: (i, 0)),
        out_shape=jax.ShapeDtypeStruct((m, d), jnp.float32),
        compiler_params=_params("parallel"),
    )(x, y_assign, y_assign, gates, mods)


def _moe_kernel(be_ref, first_ref, rows_ref, next_ref, tok_ref, asg_ref,
                h_hbm, wg_hbm, wu_hbm, wd_hbm, y_hbm,
                xbuf, obuf, stage_g, stage_u, stage_d, wg_bf, wu_bf, wd_bf, w_sems, g_sems, s_sems, *, layer):
    i = pl.program_id(0)
    n_blocks = pl.num_programs(0)
    slot = i % 2

    weights = ((wg_hbm, stage_g, wg_bf), (wu_hbm, stage_u, wu_bf), (wd_hbm, stage_d, wd_bf))

    def weight_copy(e, k):
        return pltpu.make_async_copy(weights[k][0].at[layer, e], weights[k][1], w_sems.at[k])

    def gather_copy(blk, s, r):
        return pltpu.make_async_copy(h_hbm.at[pl.ds(tok_ref[blk * MOE_BLOCK + r], 1)],
                                     xbuf.at[s, pl.ds(r, 1)], g_sems.at[s])

    def scatter_copy(blk, s, r):
        return pltpu.make_async_copy(obuf.at[s, pl.ds(r, 1)],
                                     y_hbm.at[pl.ds(asg_ref[blk * MOE_BLOCK + r], 1)], s_sems.at[s])

    def for_rows(blk, fn, round_up=False):
        n = rows_ref[blk]

        def body(c, carry):
            for u in range(DMA_UNROLL):
                fn(c * DMA_UNROLL + u)
            return carry

        def tail(r, carry):
            fn(r)
            return carry

        if round_up:
            lax.fori_loop(0, (n + DMA_UNROLL - 1) // DMA_UNROLL, body, 0)
        else:
            lax.fori_loop(0, n // DMA_UNROLL, body, 0)
            lax.fori_loop(n // DMA_UNROLL * DMA_UNROLL, n, tail, 0)

    @pl.when(i == 0)
    def _():
        xbuf[...] = jnp.zeros_like(xbuf)
        for_rows(0, lambda r: gather_copy(0, 0, r).start(), round_up=True)
        for k in range(len(weights)):
            weight_copy(be_ref[0], k).start(priority=1)

    @pl.when(i + 1 < n_blocks)
    def _():
        for_rows(i + 1, lambda r: gather_copy(i + 1, 1 - slot, r).start(), round_up=True)

    @pl.when(first_ref[i] == 1)
    def _():
        for k, (_, stage, w_bf) in enumerate(weights):
            weight_copy(be_ref[i], k).wait()
            w_bf[...] = stage[...].astype(jnp.bfloat16)

            @pl.when(next_ref[i] >= 0)
            def _():
                weight_copy(next_ref[i], k).start(priority=1)

    @pl.when(i >= 2)
    def _():
        for_rows(i - 2, lambda r: scatter_copy(i - 2, slot, r).wait())

    @pl.when(rows_ref[i] > 0)
    def _():
        for_rows(i, lambda r: gather_copy(i, slot, r).wait(), round_up=True)
        x = _unpack_rows(xbuf[slot]).astype(jnp.bfloat16)
        g = jnp.dot(x, wg_bf[...], preferred_element_type=jnp.float32)
        u = jnp.dot(x, wu_bf[...], preferred_element_type=jnp.float32)
        a = (g * jax.nn.sigmoid(g) * u).astype(jnp.bfloat16)
        obuf[slot] = _pack_rows(jnp.dot(a, wd_bf[...], preferred_element_type=jnp.float32))
        for_rows(i, lambda r: scatter_copy(i, slot, r).start())

    @pl.when(i == n_blocks - 1)
    def _():
        @pl.when(i >= 1)
        def _():
            for_rows(i - 1, lambda r: scatter_copy(i - 1, 1 - slot, r).wait())
        for_rows(i, lambda r: scatter_copy(i, slot, r).wait())


def _moe_ffn(h_packed, layout, w_gate, w_up, w_down, layer):
    block_expert, block_first, block_rows, block_next, row_token, row_assign = layout
    t, d2 = h_packed.shape
    d = 2 * d2
    n_blocks = block_expert.shape[0]
    grid_spec = pltpu.PrefetchScalarGridSpec(
        num_scalar_prefetch=6,
        grid=(n_blocks,),
        in_specs=[_any_spec(), _any_spec(), _any_spec(), _any_spec()],
        out_specs=_any_spec(),
        scratch_shapes=[
            pltpu.VMEM((2, MOE_BLOCK, d2), jnp.uint32),
            pltpu.VMEM((2, MOE_BLOCK, d2), jnp.uint32),
            pltpu.VMEM((d, MOE_FF), jnp.float32),
            pltpu.VMEM((d, MOE_FF), jnp.float32),
            pltpu.VMEM((MOE_FF, d), jnp.float32),
            pltpu.VMEM((d, MOE_FF), jnp.bfloat16),
            pltpu.VMEM((d, MOE_FF), jnp.bfloat16),
            pltpu.VMEM((MOE_FF, d), jnp.bfloat16),
            pltpu.SemaphoreType.DMA((3,)),
            pltpu.SemaphoreType.DMA((2,)),
            pltpu.SemaphoreType.DMA((2,)),
        ],
    )
    return pl.pallas_call(
        functools.partial(_moe_kernel, layer=layer),
        name="moe_ffn",
        grid_spec=grid_spec,
        out_shape=jax.ShapeDtypeStruct((t * MOE_TOP_K, d2), jnp.uint32),
        compiler_params=_params("arbitrary"),
    )(block_expert, block_first, block_rows, block_next, row_token, row_assign,
      h_packed, w_gate, w_up, w_down)


def _route(logits):
    t = logits.shape[0]
    g_logits = logits[:, :MOE_GROUPS]
    g_sel = jnp.argmax(g_logits, axis=-1)
    g_prob = jnp.take_along_axis(jax.nn.softmax(g_logits, axis=-1), g_sel[:, None], axis=1)
    e_logits = logits[:, MOE_GROUPS:MOE_GROUPS + N_EXPERTS].reshape(t, MOE_GROUPS, MOE_PER_GROUP)
    e_in_group = jnp.take_along_axis(e_logits, g_sel[:, None, None], axis=1)[:, 0]
    top_val, top_idx = lax.top_k(e_in_group, MOE_TOP_K)
    gates = g_prob * jax.nn.softmax(top_val, axis=-1)
    expert = (g_sel[:, None] * MOE_PER_GROUP + top_idx).astype(jnp.int32)
    return gates, expert


def _moe_layout(expert):
    n_assign = expert.size
    e_flat = expert.reshape(-1)
    onehot = (e_flat[:, None] == jnp.arange(N_EXPERTS, dtype=jnp.int32)[None, :]).astype(jnp.int32)
    csum = jnp.cumsum(onehot, axis=0)
    rank = jnp.take_along_axis(csum, e_flat[:, None], axis=1)[:, 0] - 1
    counts = csum[-1]
    padded = (counts + MOE_BLOCK - 1) // MOE_BLOCK * MOE_BLOCK
    p_ends = jnp.cumsum(padded)
    p_starts = p_ends - padded
    pos = (p_starts[e_flat] + rank).astype(jnp.int32)
    n_blocks = -(-n_assign // MOE_BLOCK) + N_EXPERTS
    n_rows = n_blocks * MOE_BLOCK
    assign = jnp.arange(n_assign, dtype=jnp.int32)
    row_assign = jnp.zeros((n_rows,), jnp.int32).at[pos].set(assign)
    row_token = row_assign // MOE_TOP_K
    row_dest = (row_assign % MOE_TOP_K) * (n_assign // MOE_TOP_K) + row_token
    block_start = jnp.arange(n_blocks, dtype=jnp.int32) * MOE_BLOCK
    block_expert = jnp.minimum(jnp.searchsorted(p_ends, block_start, side='right'),
                               N_EXPERTS - 1).astype(jnp.int32)
    valid = block_start < p_ends[-1]
    real_end = (p_starts + counts)[block_expert]
    block_rows = jnp.where(valid, jnp.clip(real_end - block_start, 0, MOE_BLOCK), 0).astype(jnp.int32)
    prev = jnp.concatenate([jnp.full((1,), -1, jnp.int32), block_expert[:-1]])
    block_first = (valid & (block_expert != prev)).astype(jnp.int32)
    after = jnp.minimum(p_ends[block_expert] // MOE_BLOCK, n_blocks - 1)
    has_next = p_ends[block_expert] < p_ends[-1]
    block_next = jnp.where(has_next, block_expert[after], -1).astype(jnp.int32)
    return block_expert, block_first, block_rows, block_next, row_token, row_dest


def _hier_moe(x, h_packed, logits, mods, gate_chunk, w_gate, w_up, w_down, layer):
    gates, expert = _route(logits)
    y_assign = _moe_ffn(h_packed, _moe_layout(expert), w_gate, w_up, w_down, layer)
    return _moe_combine(x, y_assign, gates, mods, gate_chunk)


def _log_sigmoid(x):
    return jnp.minimum(x, 0.0) - jnp.log(1.0 + jnp.exp(-jnp.abs(x)))


def _softplus(x):
    return jnp.maximum(x, 0.0) + jnp.log(1.0 + jnp.exp(-jnp.abs(x)))


def _mlstm_kernel(*refs, seq, has_init, emit_state, has_prev):
    f32, bf16 = jnp.float32, jnp.bfloat16
    it = iter(refs)
    q_ref, k_ref, v_ref, o_ref, gc_ref, gr_ref, nw_ref = (next(it) for _ in range(7))
    if has_init:
        c0_ref, n0_ref, m0_ref = next(it), next(it), next(it)
    if has_prev:
        cprev_ref, nprev_ref, mprev_ref = next(it), next(it), next(it)
    y_ref = next(it)
    if emit_state:
        cout_ref, nout_ref, mout_ref = next(it), next(it), next(it)
    hacc, c_sc, n_sc, m_sc = next(it), next(it), next(it), next(it)

    L = MLSTM_CHUNK
    nc = seq // L
    use_inter = has_init or nc > 1
    scale = MLSTM_DQK ** -0.5
    row = lax.broadcasted_iota(jnp.int32, (L, L), 0)
    col = lax.broadcasted_iota(jnp.int32, (L, L), 1)

    for d in range(2):
        tri = (col <= row) if d == 0 else (col >= row)
        tri_t = (row <= col) if d == 0 else (row >= col)
        if has_init:
            c_sc[...] = c0_ref[d]
            n_sc[...] = n0_ref[d]
            m_sc[...] = m0_ref[d]
        else:
            c_sc[...] = jnp.zeros_like(c_sc)
            n_sc[...] = jnp.zeros_like(n_sc)
            m_sc[...] = jnp.zeros_like(m_sc)

        def chunk(ci, carry, d=d, tri=tri, tri_t=tri_t):
            c = ci if d == 0 else nc - 1 - ci
            rows = _chunk_rows(c, L)
            q, k, v = q_ref[rows, :], k_ref[rows, :], v_ref[rows, :]
            gcol, grow = gc_ref[c], gr_ref[c]
            i_col = gcol[:, 2 * d:2 * d + 1]
            f_col = _log_sigmoid(gcol[:, 2 * d + 1:2 * d + 2])
            i_row = grow[2 * d:2 * d + 1, :]
            f_row = _log_sigmoid(grow[2 * d + 1:2 * d + 2, :])
            b_col = jnp.sum(jnp.where(tri, f_row, 0.0), axis=1, keepdims=True)
            b_row = jnp.sum(jnp.where(tri_t, f_col, 0.0), axis=0, keepdims=True)
            b_end = jnp.sum(f_row, axis=1, keepdims=True)
            m_prev = m_sc[...]
            log_d = jnp.where(tri, b_col - b_row + i_row, -jnp.inf)
            log_inter = b_col + m_prev
            m_t = jnp.maximum(log_inter, jnp.max(log_d, axis=1, keepdims=True))
            w_intra = jnp.exp(log_d - m_t)
            qk = lax.dot_general(q, k, (((1,), (1,)), ((), ())), preferred_element_type=f32) * (scale * w_intra)
            num = jnp.dot(qk.astype(bf16), v, preferred_element_type=f32)
            den = jnp.sum(qk, axis=1, keepdims=True)
            if use_inter:
                w_inter = jnp.exp(log_inter - m_t) * scale
                num = num + w_inter * jnp.dot(q, c_sc[...].astype(bf16), preferred_element_type=f32)
                den = den + w_inter * jnp.sum(q.astype(f32) * n_sc[...], axis=1, keepdims=True)
            h = num / jnp.maximum(jnp.abs(den), jnp.exp(-m_t))
            if d == 0:
                hacc[rows, :] = h
            else:
                hacc[rows, :] = hacc[rows, :] + h
            if emit_state or nc > 1:
                log_w = b_end - b_col + i_col
                m_new = jnp.maximum(b_end + m_prev, jnp.max(log_w, axis=0, keepdims=True))
                kw = k.astype(f32) * jnp.exp(log_w - m_new)
                decay = jnp.exp(b_end + m_prev - m_new)
                c_sc[...] = decay * c_sc[...] + jnp.dot(kw.T.astype(bf16), v, preferred_element_type=f32)
                n_sc[...] = decay * n_sc[...] + jnp.sum(kw, axis=0, keepdims=True)
                m_sc[...] = m_new
            return carry

        if nc == 1:
            chunk(0, 0)
        else:
            lax.fori_loop(0, nc, chunk, 0)
        if emit_state:
            last = cout_ref.shape[0] - 1
            cout_ref[last, d] = c_sc[...]
            nout_ref[last, d] = n_sc[...]
            mout_ref[last, d] = m_sc[...]

    if has_prev:
        n_prev = cprev_ref.shape[0]
        cout_ref[0:n_prev] = cprev_ref[...]
        nout_ref[0:n_prev] = nprev_ref[...]
        mout_ref[0:n_prev] = mprev_ref[...]

    hs = hacc[...]
    hn = hs * lax.rsqrt(jnp.mean(hs * hs, axis=-1, keepdims=True) + EPS) * nw_ref[...]
    y_ref[...] = (hn * jax.nn.sigmoid(o_ref[...].astype(f32))).astype(y_ref.dtype)


def _mlstm_seq(proj, gates, norm_w, layer, *, row0, nb, seq, init=None, state_prev=None, emit_state=False):
    f32 = jnp.float32
    H, L = MLSTM_HEADS, MLSTM_CHUNK
    nc = seq // L
    rb0 = row0 // seq
    g = gates[row0:row0 + nb * seq].reshape(nb, nc, L, 2, 2, H)
    g_col = g.transpose(0, 5, 1, 2, 3, 4).reshape(nb, H, nc, L, 4)
    g_row = g.transpose(0, 5, 1, 3, 4, 2).reshape(nb, H, nc, 4, L)
    qb, vb = MLSTM_DQK, MLSTM_DV
    in_specs = [
        pl.BlockSpec((seq, qb), lambda b, h: (rb0 + b, h)),
        pl.BlockSpec((seq, qb), lambda b, h: (rb0 + b, H + h)),
        pl.BlockSpec((seq, vb), lambda b, h: (rb0 + b, H + h)),
        pl.BlockSpec((seq, vb), lambda b, h: (rb0 + b, 2 * H + h)),
        pl.BlockSpec((None, None, nc, L, 4), lambda b, h: (b, h, 0, 0, 0)),
        pl.BlockSpec((None, None, nc, 4, L), lambda b, h: (b, h, 0, 0, 0)),
        pl.BlockSpec((None, None, 1, vb), lambda b, h: (layer, h, 0, 0)),
    ]
    args = [proj, proj, proj, proj, g_col, g_row, norm_w.reshape(-1, H, 1, vb)]
    has_init = init is not None
    if has_init:
        c0, n0, m0 = init
        in_specs += [
            pl.BlockSpec((None, None, 2, None, qb, vb), lambda b, h: (b, layer, 0, h, 0, 0)),
            pl.BlockSpec((None, None, 2, None, 1, qb), lambda b, h: (b, layer, 0, h, 0, 0)),
            pl.BlockSpec((None, None, 2, None, 1, 1), lambda b, h: (b, layer, 0, h, 0, 0)),
        ]
        args += [c0, n0.reshape(*n0.shape[:4], 1, qb), m0.reshape(*m0.shape[:4], 1, 1)]
    out_specs = [pl.BlockSpec((seq, vb), lambda b, h: (b, h))]
    out_shape = [jax.ShapeDtypeStruct((nb * seq, H * vb), jnp.bfloat16)]
    has_prev = state_prev is not None
    if emit_state:
        tails = ((qb, vb), (1, qb), (1, 1))

        def state_specs(n_slots):
            return [pl.BlockSpec((None, n_slots, 2, None, *t), lambda b, h: (b, 0, 0, h, 0, 0)) for t in tails]

        if has_prev:
            assert all(a.shape[1] == layer for a in state_prev)
            in_specs += state_specs(layer)
            args += list(state_prev)
        out_specs += state_specs(layer + 1)
        out_shape += [jax.ShapeDtypeStruct((nb, layer + 1, 2, H, *t), f32) for t in tails]
    return pl.pallas_call(
        functools.partial(_mlstm_kernel, seq=seq, has_init=has_init, emit_state=emit_state,
                          has_prev=has_prev and emit_state),
        name=f"mlstm_s{seq}",
        grid=(nb, H),
        in_specs=in_specs,
        out_specs=out_specs,
        out_shape=out_shape,
        scratch_shapes=[
            pltpu.VMEM((seq, vb), f32),
            pltpu.VMEM((qb, vb), f32),
            pltpu.VMEM((1, qb), f32),
            pltpu.VMEM((1, 1), f32),
        ],
        compiler_params=_params("arbitrary", "arbitrary"),
    )(*args)


def _conv_silu(x, w_ref, b_ref):
    s = x.shape[0]
    t = lax.broadcasted_iota(jnp.int32, x.shape, 0)
    y = (w_ref[0:1, :] * jnp.where(t >= 2, pltpu.roll(x, 2, 0), 0.0)
         + w_ref[1:2, :] * jnp.where(t >= 1, pltpu.roll(x, 1, 0), 0.0)
         + w_ref[2:3, :] * x
         + w_ref[3:4, :] * jnp.where(t < s - 1, pltpu.roll(x, s - 1, 0), 0.0)
         + b_ref[...])
    return y * jax.nn.sigmoid(y)


def _ssd_kernel(*refs, seq, has_init, emit_state):
    f32, bf16 = jnp.float32, jnp.bfloat16
    it = iter(refs)
    (x_ref, b_ref, c_ref, z_ref, dtc_ref, dtr_ref, dtbc_ref, dtbr_ref, alc_ref, alr_ref,
     cwx_ref, cbx_ref, cwb_ref, cbb_ref, cwc_ref, cbc_ref, dsk_ref, nw_ref) = (next(it) for _ in range(18))
    if has_init:
        h0_ref = next(it)
    y_ref = next(it)
    if emit_state:
        hout_ref = next(it)
    xc_sc, bc_sc, cc_sc, yacc, hst = (next(it) for _ in range(5))

    L = SSD_CHUNK
    nc = seq // L
    E, P = SSD_GROUP_HEADS, SSD_HEAD_DIM
    n_pair = E // 2
    W = 2 * P

    for p in range(n_pair):
        cols = slice(p * W, (p + 1) * W)
        xc_sc[:, cols] = _conv_silu(x_ref[:, cols].astype(f32), cwx_ref.at[:, cols], cbx_ref.at[:, cols])
    bc_sc[...] = _conv_silu(b_ref[...].astype(f32), cwb_ref, cbb_ref).astype(bf16)
    cc_sc[...] = _conv_silu(c_ref[...].astype(f32), cwc_ref, cbc_ref).astype(bf16)

    row = lax.broadcasted_iota(jnp.int32, (L, L), 0)
    col = lax.broadcasted_iota(jnp.int32, (L, L), 1)
    lo = lax.broadcasted_iota(jnp.int32, (L, W), 1) < P
    lo_rows = lax.broadcasted_iota(jnp.int32, (W, SSD_STATE), 0) < P
    a_col = -jnp.exp(alc_ref[...])
    a_row = -jnp.exp(alr_ref[...])

    for d in range(2):
        tri = (col <= row) if d == 0 else (col >= row)
        tri_f = tri.astype(f32)
        tri_tf = ((row <= col) if d == 0 else (row >= col)).astype(f32)
        if has_init:
            hst[...] = h0_ref[d]
        else:
            hst[...] = jnp.zeros_like(hst)

        def chunk(ci, carry, d=d, tri=tri, tri_f=tri_f, tri_tf=tri_tf):
            c = ci if d == 0 else nc - 1 - ci
            rows = _chunk_rows(c, L)
            dt_c = _softplus(dtc_ref[c] + dtbc_ref[...])
            dt_r = _softplus(dtr_ref[c] + dtbr_ref[...])
            cs_c = jnp.dot(tri_f, dt_c * a_col, preferred_element_type=f32, precision=lax.Precision.HIGHEST)
            cs_r = jnp.dot(dt_r * a_row, tri_tf, preferred_element_type=f32, precision=lax.Precision.HIGHEST)
            cs_e = cs_c[L - 1:L, :] if d == 0 else cs_c[0:1, :]
            bm, cm = bc_sc[rows, :], cc_sc[rows, :]
            cb = lax.dot_general(cm, bm, (((1,), (1,)), ((), ())), preferred_element_type=f32)
            for p in range(n_pair):
                cols = slice(p * W, (p + 1) * W)
                ia, ib = d * E + 2 * p, d * E + 2 * p + 1
                xdt = xc_sc[rows, cols] * jnp.where(lo, dt_c[:, ia:ia + 1], dt_c[:, ib:ib + 1])
                xdt_bf = xdt.astype(bf16)
                cs_ta, cs_tb = cs_c[:, ia:ia + 1], cs_c[:, ib:ib + 1]
                wa = (cb * jnp.exp(jnp.where(tri, cs_ta - cs_r[ia:ia + 1, :], -jnp.inf))).astype(bf16)
                wb = (cb * jnp.exp(jnp.where(tri, cs_tb - cs_r[ib:ib + 1, :], -jnp.inf))).astype(bf16)
                y = jnp.where(lo, jnp.dot(wa, xdt_bf, preferred_element_type=f32),
                              jnp.dot(wb, xdt_bf, preferred_element_type=f32))
                hrows = slice(p * W, (p + 1) * W)
                hp = hst[hrows, :]
                if has_init or nc > 1:
                    y_inter = lax.dot_general(cm, hp.astype(bf16), (((1,), (1,)), ((), ())),
                                              preferred_element_type=f32)
                    y = y + jnp.where(lo, jnp.exp(cs_ta), jnp.exp(cs_tb)) * y_inter
                if d == 0:
                    yacc[rows, cols] = y
                else:
                    yacc[rows, cols] = yacc[rows, cols] + y
                if emit_state or nc > 1:
                    ea, eb = cs_e[:, ia:ia + 1], cs_e[:, ib:ib + 1]
                    xw = xdt * jnp.where(lo, jnp.exp(ea - cs_ta), jnp.exp(eb - cs_tb))
                    contrib = jnp.dot(xw.T.astype(bf16), bm, preferred_element_type=f32)
                    hst[hrows, :] = jnp.where(lo_rows, jnp.exp(ea), jnp.exp(eb)) * hp + contrib
            return carry

        if nc == 1:
            chunk(0, 0)
        else:
            lax.fori_loop(0, nc, chunk, 0)
        if emit_state:
            hout_ref[d] = hst[...]

    y = yacc[...] + dsk_ref[...] * xc_sc[...]
    z = z_ref[...].astype(f32)
    y = y * (z * jax.nn.sigmoid(z))
    y = y * lax.rsqrt(jnp.mean(y * y, axis=-1, keepdims=True) + EPS) * nw_ref[...]
    y_ref[...] = y.astype(y_ref.dtype)


def _ssd_seq(zx, dt_raw, conv_w, conv_b, dt_bias, a_log, d_skip, norm_w, layer, *, row0, nb, seq,
             init=None, emit_state=False):
    f32 = jnp.float32
    G, E, L, N = SSD_GROUPS, SSD_GROUP_HEADS, SSD_CHUNK, SSD_STATE
    gc = SSD_GROUP_COLS
    nc = seq // L
    rb0 = row0 // seq
    dt = dt_raw[row0:row0 + nb * seq].reshape(nb, nc, L, 2, G, E)
    dt_col = dt.transpose(0, 4, 1, 2, 3, 5).reshape(nb, G, nc, L, 2 * E)
    dt_row = dt.transpose(0, 4, 1, 3, 5, 2).reshape(nb, G, nc, 2 * E, L)

    def per_group(a):
        a = a.astype(f32).reshape(2, G, E).transpose(1, 0, 2).reshape(G, 2 * E)
        return a[:, None, :], a[:, :, None]

    dtb_c, dtb_r = per_group(dt_bias)
    al_c, al_r = per_group(a_log)
    x_cb = SSD_INNER // gc
    bc0 = 2 * SSD_INNER // N
    cw = conv_w.astype(f32)
    cbias = conv_b.astype(f32).reshape(1, -1)
    dsk = jnp.repeat(d_skip.astype(f32), SSD_HEAD_DIM).reshape(1, SSD_INNER)
    in_specs = [
        pl.BlockSpec((seq, gc), lambda b, g: (rb0 + b, x_cb + g)),
        pl.BlockSpec((seq, N), lambda b, g: (rb0 + b, bc0 + g)),
        pl.BlockSpec((seq, N), lambda b, g: (rb0 + b, bc0 + G + g)),
        pl.BlockSpec((seq, gc), lambda b, g: (rb0 + b, g)),
        pl.BlockSpec((None, None, nc, L, 2 * E), lambda b, g: (b, g, 0, 0, 0)),
        pl.BlockSpec((None, None, nc, 2 * E, L), lambda b, g: (b, g, 0, 0, 0)),
        pl.BlockSpec((None, 1, 2 * E), lambda b, g: (g, 0, 0)),
        pl.BlockSpec((None, 2 * E, 1), lambda b, g: (g, 0, 0)),
        pl.BlockSpec((None, 1, 2 * E), lambda b, g: (g, 0, 0)),
        pl.BlockSpec((None, 2 * E, 1), lambda b, g: (g, 0, 0)),
        pl.BlockSpec((SSD_CONV, gc), lambda b, g: (0, g)),
        pl.BlockSpec((1, gc), lambda b, g: (0, g)),
        pl.BlockSpec((SSD_CONV, N), lambda b, g: (0, SSD_INNER // N + g)),
        pl.BlockSpec((1, N), lambda b, g: (0, SSD_INNER // N + g)),
        pl.BlockSpec((SSD_CONV, N), lambda b, g: (0, SSD_INNER // N + G + g)),
        pl.BlockSpec((1, N), lambda b, g: (0, SSD_INNER // N + G + g)),
        pl.BlockSpec((1, gc), lambda b, g: (0, g)),
        pl.BlockSpec((1, gc), lambda b, g: (0, g)),
    ]
    args = [zx, zx, zx, zx, dt_col, dt_row, dtb_c, dtb_r, al_c, al_r,
            cw, cbias, cw, cbias, cw, cbias, dsk, norm_w.astype(f32).reshape(1, SSD_INNER)]
    has_init = init is not None
    state_block = pl.BlockSpec((None, None, 2, None, E * SSD_HEAD_DIM, N), lambda b, g: (b, layer, 0, g, 0, 0))
    if has_init:
        in_specs.append(state_block)
        args.append(init.reshape(*init.shape[:3], G, E * SSD_HEAD_DIM, N))
    out_specs = [pl.BlockSpec((seq, gc), lambda b, g: (b, g))]
    out_shape = [jax.ShapeDtypeStruct((nb * seq, SSD_INNER), jnp.bfloat16)]
    if emit_state:
        out_specs.append(state_block)
        out_shape.append(jax.ShapeDtypeStruct((nb, 1, 2, G, E * SSD_HEAD_DIM, N), f32))
    return pl.pallas_call(
        functools.partial(_ssd_kernel, seq=seq, has_init=has_init, emit_state=emit_state),
        name=f"ssd_s{seq}",
        grid=(nb, G),
        in_specs=in_specs,
        out_specs=out_specs,
        out_shape=out_shape,
        scratch_shapes=[
            pltpu.VMEM((seq, gc), f32),
            pltpu.VMEM((seq, N), jnp.bfloat16),
            pltpu.VMEM((seq, N), jnp.bfloat16),
            pltpu.VMEM((seq, gc), f32),
            pltpu.VMEM((E * SSD_HEAD_DIM, N), f32),
        ],
        compiler_params=_params("arbitrary", "arbitrary"),
    )(*args)


def _attn_kernel(*refs, heads, rope):
    f32, bf16 = jnp.float32, jnp.bfloat16
    if rope:
        qn_ref, qp_ref, kv_ref, kpe_ref, cos_ref, sin_ref, o_ref = refs
    else:
        qn_ref, qp_ref, kv_ref, kpe_ref, o_ref = refs
    scale = (MLA_NOPE + MLA_ROPE) ** -0.5
    kpe2 = kpe_ref[...]
    lq = qn_ref.shape[0]
    lane = lax.broadcasted_iota(jnp.int32, (lq, V7X_LANES), 1)
    nt = (((1,), (1,)), ((), ()))
    for pr in range(heads // 2):
        qp = qp_ref[:, pr * V7X_LANES:(pr + 1) * V7X_LANES]
        if rope:
            x = qp.astype(f32)
            other = jnp.where(lane % (2 * ROPE_FREQS) < ROPE_FREQS,
                              pltpu.roll(x, V7X_LANES - ROPE_FREQS, 1), pltpu.roll(x, ROPE_FREQS, 1))
            qp = (x * cos_ref[...] + other * sin_ref[...]).astype(bf16)
        for a in range(2):
            h = 2 * pr + a
            qpa = jnp.where((lane < MLA_ROPE) if a == 0 else (lane >= MLA_ROPE), qp, jnp.zeros_like(qp))
            kn = kv_ref[:, h * 2 * MLA_NOPE:h * 2 * MLA_NOPE + MLA_NOPE]
            vv = kv_ref[:, h * 2 * MLA_NOPE + MLA_NOPE:(h + 1) * 2 * MLA_NOPE]
            s = (lax.dot_general(qn_ref[:, h * MLA_NOPE:(h + 1) * MLA_NOPE], kn, nt, preferred_element_type=f32)
                 + lax.dot_general(qpa, kpe2, nt, preferred_element_type=f32)) * scale
            e = jnp.exp(s - jnp.max(s, axis=-1, keepdims=True))
            o = jnp.dot(e.astype(bf16), vv, preferred_element_type=f32) / jnp.sum(e, axis=-1, keepdims=True)
            o_ref[:, h * MLA_V:(h + 1) * MLA_V] = o.astype(o_ref.dtype)


def _attention(qn, qp, kv, kpe2, *, row0, nb, seq, key0, n_keys, q_tile, heads, tables=None):
    nq = seq // q_tile
    qb0 = row0 // q_tile
    kb0 = key0 // n_keys
    assert row0 % q_tile == 0 and key0 % n_keys == 0
    hb = MLA_HEADS // heads
    in_specs = [
        pl.BlockSpec((q_tile, heads * MLA_NOPE), lambda b, t, h: (qb0 + b * nq + t, h)),
        pl.BlockSpec((q_tile, heads * MLA_ROPE), lambda b, t, h: (qb0 + b * nq + t, h)),
        pl.BlockSpec((n_keys, heads * 2 * MLA_NOPE), lambda b, t, h: (kb0 + b, h)),
        pl.BlockSpec((n_keys, V7X_LANES), lambda b, t, h: (kb0 + b, 0)),
    ]
    args = [qn, qp, kv, kpe2]
    rope = tables is not None
    if rope:
        in_specs += [pl.BlockSpec((q_tile, V7X_LANES), lambda b, t, h: (t, 0))] * 2
        args += list(tables)
    return pl.pallas_call(
        functools.partial(_attn_kernel, heads=heads, rope=rope),
        name=f"attn_s{seq}",
        grid=(nb, nq, hb),
        in_specs=in_specs,
        out_specs=pl.BlockSpec((q_tile, heads * MLA_V), lambda b, t, h: (b * nq + t, h)),
        out_shape=jax.ShapeDtypeStruct((nb * seq, MLA_HEADS * MLA_V), jnp.bfloat16),
        compiler_params=_params("arbitrary", "arbitrary", "arbitrary"),
    )(*args)


def _rope_tables(n_tok):
    rows = n_tok // GRID_W
    row = jnp.repeat(jnp.arange(rows, dtype=jnp.float32), GRID_W)
    col = jnp.tile(jnp.arange(GRID_W, dtype=jnp.float32), rows)
    inv_freq = ROPE_THETA ** (-jnp.arange(ROPE_FREQS, dtype=jnp.float32) / ROPE_FREQS)
    ang = jnp.stack([row, col], axis=-1)[:, :, None] * inv_freq
    cos, sin = jnp.cos(ang), jnp.sin(ang)
    cos64 = jnp.concatenate([cos[:, 0], cos[:, 0], cos[:, 1], cos[:, 1]], axis=-1)
    sin64 = jnp.concatenate([-sin[:, 0], sin[:, 0], -sin[:, 1], sin[:, 1]], axis=-1)
    return jnp.tile(cos64, (1, 2)), jnp.tile(sin64, (1, 2))


def _rope_rows(x, cos64, sin64):
    xr = x.reshape(x.shape[0], 2, 2, ROPE_FREQS)
    other = jnp.stack([xr[:, :, 1], xr[:, :, 0]], axis=2).reshape(x.shape)
    return x * cos64 + other * sin64


def kernel(x_prompt, x_sample, state_mlstm_C, state_mlstm_n, state_mlstm_m, state_ssm, cache_mla_ckv, cache_mla_kpe, c, c_ctx, ada_w, ada_b, norm_mix, norm_ffn, mlstm_w_in, mlstm_b_gates, mlstm_norm, mlstm_w_out, ssd_w_in, ssd_conv_w, ssd_conv_b, ssd_dt_bias, ssd_A_log, ssd_D, ssd_norm, ssd_w_out, mla_w_in, mla_q_norm, mla_kv_norm, mla_w_qb, mla_w_kvb, mla_w_out, router_group_w, router_expert_w, moe_w_gate, moe_w_up, moe_w_down, final_norm):
    f32, bf16 = jnp.float32, jnp.bfloat16
    D = D_MODEL
    X = jnp.concatenate([x_prompt.reshape(N_PROMPT, D), x_sample.reshape(N_SAMPLE, D)], axis=0)

    cond = jnp.concatenate([c_ctx[None, :], c], axis=0)
    cond = jnp.pad(jax.nn.silu(cond), ((0, 16 - cond.shape[0]), (0, 0))).astype(bf16)

    n_mlstm = mlstm_w_in.shape[0]
    mlstm_state = None
    new_ssm, new_ckv, new_kpe = [], [], []
    for l in range(DEPTH):
        kind, j = l % N_MIXERS, l // N_MIXERS
        ada = _matmul(cond, ada_w, layer=l, tm=16, tn=512, out_dtype=f32)
        mods = (ada[:1 + DEC_BATCH] + ada_b[l]).reshape(1 + DEC_BATCH, N_MOD, 1, D)
        h = _norm(X, norm_mix[l], mods=mods, shift_chunk=0, scale_chunk=1)

        if kind == 0:
            proj = _matmul(h, mlstm_w_in, layer=j, n=12288)
            n_gate = 4 * MLSTM_HEADS
            gates = _matmul(h, mlstm_w_in[j, :, 12288:], tn=n_gate, out_dtype=f32) + mlstm_b_gates[j]
            outs = _mlstm_seq(proj, gates, mlstm_norm, j, row0=0, nb=BATCH, seq=SEQ, emit_state=True,
                              state_prev=mlstm_state)
            y_p, mlstm_state = outs[0], tuple(outs[1:])
            y_s = _mlstm_seq(proj, gates, mlstm_norm, j, row0=N_PROMPT, nb=DEC_BATCH, seq=DEC_SEQ,
                             init=(state_mlstm_C, state_mlstm_n, state_mlstm_m))[0]
            X = _matmul((y_p, y_s), mlstm_w_out, layer=j, tm=512, out_dtype=f32, res=X, mods=mods, gate_chunk=2)
        elif kind == 1:
            zx = _matmul(h, ssd_w_in, layer=j, n=18432)
            dt_raw = _matmul(h, ssd_w_in, layer=j, col0=18432, n=256, tn=256, out_dtype=f32)
            w = (ssd_conv_w[j], ssd_conv_b[j], ssd_dt_bias[j], ssd_A_log[j], ssd_D[j], ssd_norm[j])
            y_p, s_h = _ssd_seq(zx, dt_raw, *w, 0, row0=0, nb=BATCH, seq=SEQ, emit_state=True)
            y_s = _ssd_seq(zx, dt_raw, *w, j, row0=N_PROMPT, nb=DEC_BATCH, seq=DEC_SEQ, init=state_ssm)[0]
            new_ssm.append(s_h.reshape(BATCH, 1, 2, SSD_HEADS, SSD_HEAD_DIM, SSD_STATE))
            X = _matmul(jnp.concatenate([y_p, y_s], axis=0), ssd_w_out, layer=j, tm=512, tn=512, out_dtype=f32, res=X, mods=mods, gate_chunk=2,
                        single_buffer_w=True)
        else:
            lat = _matmul(h, mla_w_in, layer=j, n=2048, out_dtype=f32)
            kpe = _matmul(h, mla_w_in[j, :, 2048:], tn=MLA_ROPE, out_dtype=f32)
            hq = _norm(lat[:, :MLA_Q_RANK], mla_q_norm[j])
            ckv = _norm(lat[:, MLA_Q_RANK:], mla_kv_norm[j], out_dtype=f32)
            wq = mla_w_qb[j].reshape(MLA_Q_RANK, MLA_HEADS, MLA_NOPE + MLA_ROPE)
            qn = _matmul(hq, wq[:, :, :MLA_NOPE].reshape(MLA_Q_RANK, -1), tn=512)
            qp = _matmul(hq, wq[:, :, MLA_NOPE:].reshape(MLA_Q_RANK, -1), tn=512)
            n_keys = PAST_LEN + DEC_SEQ
            n_skeys = DEC_BATCH * n_keys
            cos2, sin2 = _rope_tables(DEC_SEQ)
            ckv_s = jnp.concatenate([cache_mla_ckv[:, j], ckv[N_PROMPT:].reshape(DEC_BATCH, DEC_SEQ, -1)], axis=1)
            ckv_all = jnp.concatenate([ckv_s.reshape(-1, MLA_KV_RANK), ckv[:N_PROMPT]], axis=0).astype(bf16)
            kv = _matmul(ckv_all, mla_w_kvb, layer=j, tn=512)
            kpe_lat = _rope_rows(kpe[N_PROMPT:], jnp.tile(cos2[:, :MLA_ROPE], (DEC_BATCH, 1)),
                                 jnp.tile(sin2[:, :MLA_ROPE], (DEC_BATCH, 1)))
            kpe_s = jnp.concatenate([cache_mla_kpe[:, j], kpe_lat.reshape(DEC_BATCH, DEC_SEQ, -1)], axis=1)
            kpe_all = jnp.concatenate([kpe_s.reshape(-1, MLA_ROPE), kpe[:N_PROMPT]], axis=0).astype(bf16)
            kpe2 = jnp.concatenate([kpe_all, kpe_all], axis=1)
            o_p = _attention(qn, qp, kv, kpe2, row0=0, nb=BATCH, seq=SEQ, key0=n_skeys, n_keys=SEQ, q_tile=SEQ,
                             heads=8)
            o_s = _attention(qn, qp, kv, kpe2, row0=N_PROMPT, nb=DEC_BATCH, seq=DEC_SEQ, key0=0,
                             n_keys=n_keys, q_tile=ATTN_Q_TILE, heads=2, tables=(cos2, sin2))
            new_ckv.append(ckv[:N_PROMPT].reshape(BATCH, 1, SEQ, MLA_KV_RANK))
            new_kpe.append(kpe[:N_PROMPT].reshape(BATCH, 1, SEQ, MLA_ROPE))
            X = _matmul(jnp.concatenate([o_p, o_s], axis=0), mla_w_out, layer=j, tm=512, tn=512, out_dtype=f32, res=X, mods=mods, gate_chunk=2,
                        single_buffer_w=True)

        router_w = jnp.pad(jnp.concatenate([router_group_w[l], router_expert_w[l]], axis=1),
                           ((0, 0), (0, ROUTER_LANES - MOE_GROUPS - N_EXPERTS)))
        hf, logits = _norm(X, norm_ffn[l], mods=mods, shift_chunk=3, scale_chunk=4, router_w=router_w,
                           out_dtype=jnp.uint32)
        X = _hier_moe(X, hf, logits, mods, 5, moe_w_gate, moe_w_up, moe_w_down, l)

    Y = _norm(X, final_norm, out_dtype=f32)
    y_prompt = Y[:N_PROMPT].reshape(BATCH, SEQ, D)
    y_sample = Y[N_PROMPT:].reshape(DEC_BATCH, DEC_SEQ, D)
    new_c, new_n, new_m = mlstm_state
    return (y_prompt, y_sample, new_c,
            new_n.reshape(BATCH, n_mlstm, 2, MLSTM_HEADS, MLSTM_DQK),
            new_m.reshape(BATCH, n_mlstm, 2, MLSTM_HEADS),
            jnp.concatenate(new_ssm, axis=1), jnp.concatenate(new_ckv, axis=1), jnp.concatenate(new_kpe, axis=1))
```

```python
import functools

import jax
import jax.numpy as jnp
from jax import lax
from jax.experimental import pallas as pl
from jax.experimental.pallas import tpu as pltpu

D_MODEL = 4096
BATCH = 32
SEQ = 256
DEPTH = 4
DEC_BATCH = 2
DEC_SEQ = 1024
PAST_LEN = 512
GRID_W = 64
N_MIXERS = 3
EPS = 1e-6

MLSTM_HEADS = 8
MLSTM_DQK = 256
MLSTM_DV = 512

SSD_INNER = 8192
SSD_HEAD_DIM = 64
SSD_HEADS = 128
SSD_GROUPS = 8
SSD_STATE = 128
SSD_CONV = 4
SSD_GROUP_HEADS = SSD_HEADS // SSD_GROUPS
SSD_GROUP_COLS = SSD_INNER // SSD_GROUPS

MLA_HEADS = 64
MLA_Q_RANK = 1536
MLA_KV_RANK = 512
MLA_NOPE = 128
MLA_ROPE = 64
MLA_V = 128
ROPE_THETA = 10000.0
ROPE_FREQS = 16

MOE_GROUPS = 8
MOE_PER_GROUP = 8
N_EXPERTS = 64
MOE_TOP_K = 2
MOE_FF = 512

N_PROMPT = BATCH * SEQ
N_SAMPLE = DEC_BATCH * DEC_SEQ
N_TOK = N_PROMPT + N_SAMPLE
N_MOD = 6

V7X_VMEM_LIMIT = 56 * 1024 * 1024
V7X_LANES = 128
MOE_BLOCK = 256
DMA_UNROLL = 16
ROUTER_LANES = V7X_LANES
MLSTM_CHUNK = 256
SSD_CHUNK = 128
ATTN_Q_TILE = 512


def _params(*sem):
    return pltpu.CompilerParams(dimension_semantics=sem, vmem_limit_bytes=V7X_VMEM_LIMIT)


def _row_group(tile, rows_per_tile):
    first_sample = N_PROMPT // rows_per_tile
    per_batch = DEC_SEQ // rows_per_tile
    return jnp.where(tile < first_sample, 0, 1 + (tile - first_sample) // per_batch)


def _any_spec():
    return pl.BlockSpec(memory_space=pl.ANY)


def _chunk_rows(c, length):
    if isinstance(c, int):
        return pl.ds(c * length, length)
    return pl.ds(pl.multiple_of(c * length, length), length)


def _mm_kernel(*refs, has_res, split_tile):
    n_x = 1 if split_tile is None else 2
    x_refs, refs = refs[:n_x], refs[n_x:]
    if has_res:
        w_ref, res_ref, gate_ref, o_ref, wbf_ref = refs
    else:
        w_ref, o_ref, wbf_ref = refs
    i = pl.program_id(1)

    @pl.when(i == 0)
    def _():
        wbf_ref[...] = w_ref[...].astype(jnp.bfloat16)

    def emit(x_ref):
        acc = jnp.dot(x_ref[...], wbf_ref[...], preferred_element_type=jnp.float32)
        if has_res:
            acc = res_ref[...] + gate_ref[...] * acc
        o_ref[...] = acc.astype(o_ref.dtype)

    if split_tile is None:
        emit(x_refs[0])
    else:
        pl.when(i < split_tile)(lambda: emit(x_refs[0]))
        pl.when(i >= split_tile)(lambda: emit(x_refs[1]))


def _matmul(x, w, *, layer=None, col0=0, n=None, tm=1024, tn=512, out_dtype=jnp.bfloat16,
            res=None, mods=None, gate_chunk=None, single_buffer_w=False):
    xs = x if isinstance(x, (tuple, list)) else (x,)
    m, k = sum(a.shape[0] for a in xs), xs[0].shape[1]
    if w.ndim == 2:
        w = w[None]
        layer = 0
    n = w.shape[2] - col0 if n is None else n
    tm = min(tm, m)
    assert all(a.shape[0] % tm == 0 for a in xs) and col0 % tn == 0 and w.shape[1] == k
    cb0 = col0 // tn
    grid = (pl.cdiv(n, tn), m // tm)
    w_mode = {"pipeline_mode": pl.Buffered(1)} if single_buffer_w else {}
    if len(xs) == 1:
        split_tile = None
        in_specs = [pl.BlockSpec((tm, k), lambda j, i: (i, 0))]
    else:
        split_tile = xs[0].shape[0] // tm
        in_specs = [pl.BlockSpec((tm, k), lambda j, i: (jnp.minimum(i, split_tile - 1), 0)),
                    pl.BlockSpec((tm, k), lambda j, i: (jnp.maximum(i - split_tile, 0), 0))]
    in_specs.append(pl.BlockSpec((None, k, tn), lambda j, i: (layer, 0, cb0 + j), **w_mode))
    args = [*xs, w]
    has_res = res is not None
    if has_res:
        in_specs += [
            pl.BlockSpec((tm, tn), lambda j, i: (i, j)),
            pl.BlockSpec((None, None, 1, tn), lambda j, i: (_row_group(i, tm), gate_chunk, 0, j)),
        ]
        args += [res, mods]
    n_out = grid[0] * tn
    return pl.pallas_call(
        functools.partial(_mm_kernel, has_res=has_res, split_tile=split_tile),
        name=f"mm_k{k}_n{n}",
        grid=grid,
        in_specs=in_specs,
        out_specs=pl.BlockSpec((tm, tn), lambda j, i: (i, j)),
        out_shape=jax.ShapeDtypeStruct((m, n_out), out_dtype),
        scratch_shapes=[pltpu.VMEM((k, tn), jnp.bfloat16)],
        compiler_params=_params("parallel", "arbitrary"),
    )(*args)


HI16 = 0xFFFF0000


def _pack_rows(y):
    u = lax.bitcast_convert_type(y.astype(jnp.bfloat16).astype(jnp.float32), jnp.uint32)
    half = y.shape[1] // 2
    return (u[:, half:] & jnp.uint32(HI16)) | (u[:, :half] >> 16)


def _unpack_rows(u):
    lo = lax.bitcast_convert_type(u << 16, jnp.float32)
    hi = lax.bitcast_convert_type(u & jnp.uint32(HI16), jnp.float32)
    return jnp.concatenate([lo, hi], axis=1)


def _norm_kernel(*refs, has_mod, has_router):
    it = iter(refs)
    x_ref, w_ref = next(it), next(it)
    shift_ref = scale_ref = wr_ref = None
    if has_mod:
        shift_ref, scale_ref = next(it), next(it)
    if has_router:
        wr_ref = next(it)
    o_ref = next(it)
    x = x_ref[...].astype(jnp.float32)
    y = x * lax.rsqrt(jnp.mean(x * x, axis=-1, keepdims=True) + EPS) * w_ref[...]
    if has_mod:
        y = y * (1.0 + scale_ref[...]) + shift_ref[...]
    o_ref[...] = _pack_rows(y) if o_ref.dtype == jnp.uint32 else y.astype(o_ref.dtype)
    if has_router:
        logits_ref = next(it)
        logits_ref[...] = jnp.dot(y, wr_ref[...], preferred_element_type=jnp.float32,
                                  precision=lax.Precision.HIGHEST)


def _norm(x, w, *, mods=None, shift_chunk=None, scale_chunk=None, router_w=None, tm=512,
          out_dtype=jnp.bfloat16):
    m, d = x.shape
    has_mod = mods is not None
    has_router = router_w is not None
    in_specs = [pl.BlockSpec((tm, d), lambda i: (i, 0)), pl.BlockSpec((1, d), lambda i: (0, 0))]
    args = [x, w.reshape(1, d)]
    if has_mod:
        in_specs += [
            pl.BlockSpec((None, None, 1, d), lambda i: (_row_group(i, tm), shift_chunk, 0, 0)),
            pl.BlockSpec((None, None, 1, d), lambda i: (_row_group(i, tm), scale_chunk, 0, 0)),
        ]
        args += [mods, mods]
    d_out = d // 2 if out_dtype == jnp.uint32 else d
    out_specs = pl.BlockSpec((tm, d_out), lambda i: (i, 0))
    out_shape = jax.ShapeDtypeStruct((m, d_out), out_dtype)
    if has_router:
        in_specs.append(pl.BlockSpec((d, ROUTER_LANES), lambda i: (0, 0)))
        args.append(router_w)
        out_specs = [out_specs, pl.BlockSpec((tm, ROUTER_LANES), lambda i: (i, 0))]
        out_shape = [out_shape, jax.ShapeDtypeStruct((m, ROUTER_LANES), jnp.float32)]
    return pl.pallas_call(
        functools.partial(_norm_kernel, has_mod=has_mod, has_router=has_router),
        name="norm_router" if has_router else "norm",
        grid=(m // tm,),
        in_specs=in_specs,
        out_specs=out_specs,
        out_shape=out_shape,
        compiler_params=_params("parallel"),
    )(*args)


def _combine_kernel(x_ref, y0_ref, y1_ref, g_ref, gate_ref, o_ref):
    g = g_ref[...]
    y = g[:, 0:1] * _unpack_rows(y0_ref[...]) + g[:, 1:2] * _unpack_rows(y1_ref[...])
    o_ref[...] = x_ref[...] + gate_ref[...] * y


def _moe_combine(x, y_assign, gates, mods, gate_chunk, tm=256):
    m, d = x.shape
    assert MOE_TOP_K == 2
    return pl.pallas_call(
        _combine_kernel,
        name="moe_combine",
        grid=(m // tm,),
        in_specs=[
            pl.BlockSpec((tm, d), lambda i: (i, 0)),
            pl.BlockSpec((tm, d // 2), lambda i: (i, 0)),
            pl.BlockSpec((tm, d // 2), lambda i: (m // tm + i, 0)),
            pl.BlockSpec((tm, MOE_TOP_K), lambda i: (i, 0)),
            pl.BlockSpec((None, None, 1, d), lambda i: (_row_group(i, tm), gate_chunk, 0, 0)),
        ],
        out_specs=pl.BlockSpec((tm, d), lambda i: (i, 0)),
        out_shape=jax.ShapeDtypeStruct((m, d), jnp.float32),
        compiler_params=_params("parallel"),
    )(x, y_assign, y_assign, gates, mods)


def _moe_kernel(be_ref, first_ref, rows_ref, next_ref, tok_ref, asg_ref,
                h_hbm, wg_hbm, wu_hbm, wd_hbm, y_hbm,
                xbuf, obuf, stage_g, stage_u, stage_d, wg_bf, wu_bf, wd_bf, w_sems, g_sems, s_sems, *, layer):
    i = pl.program_id(0)
    n_blocks = pl.num_programs(0)
    slot = i % 2

    weights = ((wg_hbm, stage_g, wg_bf), (wu_hbm, stage_u, wu_bf), (wd_hbm, stage_d, wd_bf))

    def weight_copy(e, k):
        return pltpu.make_async_copy(weights[k][0].at[layer, e], weights[k][1], w_sems.at[k])

    def gather_copy(blk, s, r):
        return pltpu.make_async_copy(h_hbm.at[pl.ds(tok_ref[blk * MOE_BLOCK + r], 1)],
                                     xbuf.at[s, pl.ds(r, 1)], g_sems.at[s])

    def scatter_copy(blk, s, r):
        return pltpu.make_async_copy(obuf.at[s, pl.ds(r, 1)],
                                     y_hbm.at[pl.ds(asg_ref[blk * MOE_BLOCK + r], 1)], s_sems.at[s])

    def for_rows(blk, fn, round_up=False):
        n = rows_ref[blk]

        def body(c, carry):
            for u in range(DMA_UNROLL):
                fn(c * DMA_UNROLL + u)
            return carry

        def tail(r, carry):
            fn(r)
            return carry

        if round_up:
            lax.fori_loop(0, (n + DMA_UNROLL - 1) // DMA_UNROLL, body, 0)
        else:
            lax.fori_loop(0, n // DMA_UNROLL, body, 0)
            lax.fori_loop(n // DMA_UNROLL * DMA_UNROLL, n, tail, 0)

    @pl.when(i == 0)
    def _():
        xbuf[...] = jnp.zeros_like(xbuf)
        for_rows(0, lambda r: gather_copy(0, 0, r).start(), round_up=True)
        for k in range(len(weights)):
            weight_copy(be_ref[0], k).start(priority=1)

    @pl.when(i + 1 < n_blocks)
    def _():
        for_rows(i + 1, lambda r: gather_copy(i + 1, 1 - slot, r).start(), round_up=True)

    @pl.when(first_ref[i] == 1)
    def _():
        for k, (_, stage, w_bf) in enumerate(weights):
            weight_copy(be_ref[i], k).wait()
            w_bf[...] = stage[...].astype(jnp.bfloat16)

            @pl.when(next_ref[i] >= 0)
            def _():
                weight_copy(next_ref[i], k).start(priority=1)

    @pl.when(i >= 2)
    def _():
        for_rows(i - 2, lambda r: scatter_copy(i - 2, slot, r).wait())

    @pl.when(rows_ref[i] > 0)
    def _():
        for_rows(i, lambda r: gather_copy(i, slot, r).wait(), round_up=True)
        x = _unpack_rows(xbuf[slot]).astype(jnp.bfloat16)
        g = jnp.dot(x, wg_bf[...], preferred_element_type=jnp.float32)
        u = jnp.dot(x, wu_bf[...], preferred_element_type=jnp.float32)
        a = (g * jax.nn.sigmoid(g) * u).astype(jnp.bfloat16)
        obuf[slot] = _pack_rows(jnp.dot(a, wd_bf[...], preferred_element_type=jnp.float32))
        for_rows(i, lambda r: scatter_copy(i, slot, r).start())

    @pl.when(i == n_blocks - 1)
    def _():
        @pl.when(i >= 1)
        def _():
            for_rows(i - 1, lambda r: scatter_copy(i - 1, 1 - slot, r).wait())
        for_rows(i, lambda r: scatter_copy(i, slot, r).wait())


def _moe_ffn(h_packed, layout, w_gate, w_up, w_down, layer):
    block_expert, block_first, block_rows, block_next, row_token, row_assign = layout
    t, d2 = h_packed.shape
    d = 2 * d2
    n_blocks = block_expert.shape[0]
    grid_spec = pltpu.PrefetchScalarGridSpec(
        num_scalar_prefetch=6,
        grid=(n_blocks,),
        in_specs=[_any_spec(), _any_spec(), _any_spec(), _any_spec()],
        out_specs=_any_spec(),
        scratch_shapes=[
            pltpu.VMEM((2, MOE_BLOCK, d2), jnp.uint32),
            pltpu.VMEM((2, MOE_BLOCK, d2), jnp.uint32),
            pltpu.VMEM((d, MOE_FF), jnp.float32),
            pltpu.VMEM((d, MOE_FF), jnp.float32),
            pltpu.VMEM((MOE_FF, d), jnp.float32),
            pltpu.VMEM((d, MOE_FF), jnp.bfloat16),
            pltpu.VMEM((d, MOE_FF), jnp.bfloat16),
            pltpu.VMEM((MOE_FF, d), jnp.bfloat16),
            pltpu.SemaphoreType.DMA((3,)),
            pltpu.SemaphoreType.DMA((2,)),
            pltpu.SemaphoreType.DMA((2,)),
        ],
    )
    return pl.pallas_call(
        functools.partial(_moe_kernel, layer=layer),
        name="moe_ffn",
        grid_spec=grid_spec,
        out_shape=jax.ShapeDtypeStruct((t * MOE_TOP_K, d2), jnp.uint32),
        compiler_params=_params("arbitrary"),
    )(block_expert, block_first, block_rows, block_next, row_token, row_assign,
      h_packed, w_gate, w_up, w_down)


def _route(logits):
    t = logits.shape[0]
    g_logits = logits[:, :MOE_GROUPS]
    g_sel = jnp.argmax(g_logits, axis=-1)
    g_prob = jnp.take_along_axis(jax.nn.softmax(g_logits, axis=-1), g_sel[:, None], axis=1)
    e_logits = logits[:, MOE_GROUPS:MOE_GROUPS + N_EXPERTS].reshape(t, MOE_GROUPS, MOE_PER_GROUP)
    e_in_group = jnp.take_along_axis(e_logits, g_sel[:, None, None], axis=1)[:, 0]
    top_val, top_idx = lax.top_k(e_in_group, MOE_TOP_K)
    gates = g_prob * jax.nn.softmax(top_val, axis=-1)
    expert = (g_sel[:, None] * MOE_PER_GROUP + top_idx).astype(jnp.int32)
    return gates, expert


def _moe_layout(expert):
    n_assign = expert.size
    e_flat = expert.reshape(-1)
    onehot = (e_flat[:, None] == jnp.arange(N_EXPERTS, dtype=jnp.int32)[None, :]).astype(jnp.int32)
    csum = jnp.cumsum(onehot, axis=0)
    rank = jnp.take_along_axis(csum, e_flat[:, None], axis=1)[:, 0] - 1
    counts = csum[-1]
    padded = (counts + MOE_BLOCK - 1) // MOE_BLOCK * MOE_BLOCK
    p_ends = jnp.cumsum(padded)
    p_starts = p_ends - padded
    pos = (p_starts[e_flat] + rank).astype(jnp.int32)
    n_blocks = -(-n_assign // MOE_BLOCK) + N_EXPERTS
    n_rows = n_blocks * MOE_BLOCK
    assign = jnp.arange(n_assign, dtype=jnp.int32)
    row_assign = jnp.zeros((n_rows,), jnp.int32).at[pos].set(assign)
    row_token = row_assign // MOE_TOP_K
    row_dest = (row_assign % MOE_TOP_K) * (n_assign // MOE_TOP_K) + row_token
    block_start = jnp.arange(n_blocks, dtype=jnp.int32) * MOE_BLOCK
    block_expert = jnp.minimum(jnp.searchsorted(p_ends, block_start, side='right'),
                               N_EXPERTS - 1).astype(jnp.int32)
    valid = block_start < p_ends[-1]
    real_end = (p_starts + counts)[block_expert]
    block_rows = jnp.where(valid, jnp.clip(real_end - block_start, 0, MOE_BLOCK), 0).astype(jnp.int32)
    prev = jnp.concatenate([jnp.full((1,), -1, jnp.int32), block_expert[:-1]])
    block_first = (valid & (block_expert != prev)).astype(jnp.int32)
    after = jnp.minimum(p_ends[block_expert] // MOE_BLOCK, n_blocks - 1)
    has_next = p_ends[block_expert] < p_ends[-1]
    block_next = jnp.where(has_next, block_expert[after], -1).astype(jnp.int32)
    return block_expert, block_first, block_rows, block_next, row_token, row_dest


def _hier_moe(x, h_packed, logits, mods, gate_chunk, w_gate, w_up, w_down, layer):
    gates, expert = _route(logits)
    y_assign = _moe_ffn(h_packed, _moe_layout(expert), w_gate, w_up, w_down, layer)
    return _moe_combine(x, y_assign, gates, mods, gate_chunk)


def _log_sigmoid(x):
    return jnp.minimum(x, 0.0) - jnp.log(1.0 + jnp.exp(-jnp.abs(x)))


def _softplus(x):
    return jnp.maximum(x, 0.0) + jnp.log(1.0 + jnp.exp(-jnp.abs(x)))


def _mlstm_kernel(*refs, seq, has_init, emit_state, has_prev):
    f32, bf16 = jnp.float32, jnp.bfloat16
    it = iter(refs)
    q_ref, k_ref, v_ref, o_ref, gc_ref, gr_ref, nw_ref = (next(it) for _ in range(7))
    if has_init:
        c0_ref, n0_ref, m0_ref = next(it), next(it), next(it)
    if has_prev:
        cprev_ref, nprev_ref, mprev_ref = next(it), next(it), next(it)
    y_ref = next(it)
    if emit_state:
        cout_ref, nout_ref, mout_ref = next(it), next(it), next(it)
    hacc, c_sc, n_sc, m_sc = next(it), next(it), next(it), next(it)

    L = MLSTM_CHUNK
    nc = seq // L
    use_inter = has_init or nc > 1
    scale = MLSTM_DQK ** -0.5
    row = lax.broadcasted_iota(jnp.int32, (L, L), 0)
    col = lax.broadcasted_iota(jnp.int32, (L, L), 1)

    for d in range(2):
        tri = (col <= row) if d == 0 else (col >= row)
        tri_t = (row <= col) if d == 0 else (row >= col)
        if has_init:
            c_sc[...] = c0_ref[d]
            n_sc[...] = n0_ref[d]
            m_sc[...] = m0_ref[d]
        else:
            c_sc[...] = jnp.zeros_like(c_sc)
            n_sc[...] = jnp.zeros_like(n_sc)
            m_sc[...] = jnp.zeros_like(m_sc)

        def chunk(ci, carry, d=d, tri=tri, tri_t=tri_t):
            c = ci if d == 0 else nc - 1 - ci
            rows = _chunk_rows(c, L)
            q, k, v = q_ref[rows, :], k_ref[rows, :], v_ref[rows, :]
            gcol, grow = gc_ref[c], gr_ref[c]
            i_col = gcol[:, 2 * d:2 * d + 1]
            f_col = _log_sigmoid(gcol[:, 2 * d + 1:2 * d + 2])
            i_row = grow[2 * d:2 * d + 1, :]
            f_row = _log_sigmoid(grow[2 * d + 1:2 * d + 2, :])
            b_col = jnp.sum(jnp.where(tri, f_row, 0.0), axis=1, keepdims=True)
            b_row = jnp.sum(jnp.where(tri_t, f_col, 0.0), axis=0, keepdims=True)
            b_end = jnp.sum(f_row, axis=1, keepdims=True)
            m_prev = m_sc[...]
            log_d = jnp.where(tri, b_col - b_row + i_row, -jnp.inf)
            log_inter = b_col + m_prev
            m_t = jnp.maximum(log_inter, jnp.max(log_d, axis=1, keepdims=True))
            w_intra = jnp.exp(log_d - m_t)
            qk = lax.dot_general(q, k, (((1,), (1,)), ((), ())), preferred_element_type=f32) * (scale * w_intra)
            num = jnp.dot(qk.astype(bf16), v, preferred_element_type=f32)
            den = jnp.sum(qk, axis=1, keepdims=True)
            if use_inter:
                w_inter = jnp.exp(log_inter - m_t) * scale
                num = num + w_inter * jnp.dot(q, c_sc[...].astype(bf16), preferred_element_type=f32)
                den = den + w_inter * jnp.sum(q.astype(f32) * n_sc[...], axis=1, keepdims=True)
            h = num / jnp.maximum(jnp.abs(den), jnp.exp(-m_t))
            if d == 0:
                hacc[rows, :] = h
            else:
                hacc[rows, :] = hacc[rows, :] + h
            if emit_state or nc > 1:
                log_w = b_end - b_col + i_col
                m_new = jnp.maximum(b_end + m_prev, jnp.max(log_w, axis=0, keepdims=True))
                kw = k.astype(f32) * jnp.exp(log_w - m_new)
                decay = jnp.exp(b_end + m_prev - m_new)
                c_sc[...] = decay * c_sc[...] + jnp.dot(kw.T.astype(bf16), v, preferred_element_type=f32)
                n_sc[...] = decay * n_sc[...] + jnp.sum(kw, axis=0, keepdims=True)
                m_sc[...] = m_new
            return carry

        if nc == 1:
            chunk(0, 0)
        else:
            lax.fori_loop(0, nc, chunk, 0)
        if emit_state:
            last = cout_ref.shape[0] - 1
            cout_ref[last, d] = c_sc[...]
            nout_ref[last, d] = n_sc[...]
            mout_ref[last, d] = m_sc[...]

    if has_prev:
        n_prev = cprev_ref.shape[0]
        cout_ref[0:n_prev] = cprev_ref[...]
        nout_ref[0:n_prev] = nprev_ref[...]
        mout_ref[0:n_prev] = mprev_ref[...]

    hs = hacc[...]
    hn = hs * lax.rsqrt(jnp.mean(hs * hs, axis=-1, keepdims=True) + EPS) * nw_ref[...]
    y_ref[...] = (hn * jax.nn.sigmoid(o_ref[...].astype(f32))).astype(y_ref.dtype)


def _mlstm_seq(proj, gates, norm_w, layer, *, row0, nb, seq, init=None, state_prev=None, emit_state=False):
    f32 = jnp.float32
    H, L = MLSTM_HEADS, MLSTM_CHUNK
    nc = seq // L
    rb0 = row0 // seq
    g = gates[row0:row0 + nb * seq].reshape(nb, nc, L, 2, 2, H)
    g_col = g.transpose(0, 5, 1, 2, 3, 4).reshape(nb, H, nc, L, 4)
    g_row = g.transpose(0, 5, 1, 3, 4, 2).reshape(nb, H, nc, 4, L)
    qb, vb = MLSTM_DQK, MLSTM_DV
    in_specs = [
        pl.BlockSpec((seq, qb), lambda b, h: (rb0 + b, h)),
        pl.BlockSpec((seq, qb), lambda b, h: (rb0 + b, H + h)),
        pl.BlockSpec((seq, vb), lambda b, h: (rb0 + b, H + h)),
        pl.BlockSpec((seq, vb), lambda b, h: (rb0 + b, 2 * H + h)),
        pl.BlockSpec((None, None, nc, L, 4), lambda b, h: (b, h, 0, 0, 0)),
        pl.BlockSpec((None, None, nc, 4, L), lambda b, h: (b, h, 0, 0, 0)),
        pl.BlockSpec((None, None, 1, vb), lambda b, h: (layer, h, 0, 0)),
    ]
    args = [proj, proj, proj, proj, g_col, g_row, norm_w.reshape(-1, H, 1, vb)]
    has_init = init is not None
    if has_init:
        c0, n0, m0 = init
        in_specs += [
            pl.BlockSpec((None, None, 2, None, qb, vb), lambda b, h: (b, layer, 0, h, 0, 0)),
            pl.BlockSpec((None, None, 2, None, 1, qb), lambda b, h: (b, layer, 0, h, 0, 0)),
            pl.BlockSpec((None, None, 2, None, 1, 1), lambda b, h: (b, layer, 0, h, 0, 0)),
        ]
        args += [c0, n0.reshape(*n0.shape[:4], 1, qb), m0.reshape(*m0.shape[:4], 1, 1)]
    out_specs = [pl.BlockSpec((seq, vb), lambda b, h: (b, h))]
    out_shape = [jax.ShapeDtypeStruct((nb * seq, H * vb), jnp.bfloat16)]
    has_prev = state_prev is not None
    if emit_state:
        tails = ((qb, vb), (1, qb), (1, 1))

        def state_specs(n_slots):
            return [pl.BlockSpec((None, n_slots, 2, None, *t), lambda b, h: (b, 0, 0, h, 0, 0)) for t in tails]

        if has_prev:
            assert all(a.shape[1] == layer for a in state_prev)
            in_specs += state_specs(layer)
            args += list(state_prev)
        out_specs += state_specs(layer + 1)
        out_shape += [jax.ShapeDtypeStruct((nb, layer + 1, 2, H, *t), f32) for t in tails]
    return pl.pallas_call(
        functools.partial(_mlstm_kernel, seq=seq, has_init=has_init, emit_state=emit_state,
                          has_prev=has_prev and emit_state),
        name=f"mlstm_s{seq}",
        grid=(nb, H),
        in_specs=in_specs,
        out_specs=out_specs,
        out_shape=out_shape,
        scratch_shapes=[
            pltpu.VMEM((seq, vb), f32),
            pltpu.VMEM((qb, vb), f32),
            pltpu.VMEM((1, qb), f32),
            pltpu.VMEM((1, 1), f32),
        ],
        compiler_params=_params("arbitrary", "arbitrary"),
    )(*args)


def _conv_silu(x, w_ref, b_ref):
    s = x.shape[0]
    t = lax.broadcasted_iota(jnp.int32, x.shape, 0)
    y = (w_ref[0:1, :] * jnp.where(t >= 2, pltpu.roll(x, 2, 0), 0.0)
         + w_ref[1:2, :] * jnp.where(t >= 1, pltpu.roll(x, 1, 0), 0.0)
         + w_ref[2:3, :] * x
         + w_ref[3:4, :] * jnp.where(t < s - 1, pltpu.roll(x, s - 1, 0), 0.0)
         + b_ref[...])
    return y * jax.nn.sigmoid(y)


def _ssd_kernel(*refs, seq, has_init, emit_state):
    f32, bf16 = jnp.float32, jnp.bfloat16
    it = iter(refs)
    (x_ref, b_ref, c_ref, z_ref, dtc_ref, dtr_ref, dtbc_ref, dtbr_ref, alc_ref, alr_ref,
     cwx_ref, cbx_ref, cwb_ref, cbb_ref, cwc_ref, cbc_ref, dsk_ref, nw_ref) = (next(it) for _ in range(18))
    if has_init:
        h0_ref = next(it)
    y_ref = next(it)
    if emit_state:
        hout_ref = next(it)
    xc_sc, bc_sc, cc_sc, yacc, hst = (next(it) for _ in range(5))

    L = SSD_CHUNK
    nc = seq // L
    E, P = SSD_GROUP_HEADS, SSD_HEAD_DIM
    n_pair = E // 2
    W = 2 * P

    for p in range(n_pair):
        cols = slice(p * W, (p + 1) * W)
        xc_sc[:, cols] = _conv_silu(x_ref[:, cols].astype(f32), cwx_ref.at[:, cols], cbx_ref.at[:, cols])
    bc_sc[...] = _conv_silu(b_ref[...].astype(f32), cwb_ref, cbb_ref).astype(bf16)
    cc_sc[...] = _conv_silu(c_ref[...].astype(f32), cwc_ref, cbc_ref).astype(bf16)

    row = lax.broadcasted_iota(jnp.int32, (L, L), 0)
    col = lax.broadcasted_iota(jnp.int32, (L, L), 1)
    lo = lax.broadcasted_iota(jnp.int32, (L, W), 1) < P
    lo_rows = lax.broadcasted_iota(jnp.int32, (W, SSD_STATE), 0) < P
    a_col = -jnp.exp(alc_ref[...])
    a_row = -jnp.exp(alr_ref[...])

    for d in range(2):
        tri = (col <= row) if d == 0 else (col >= row)
        tri_f = tri.astype(f32)
        tri_tf = ((row <= col) if d == 0 else (row >= col)).astype(f32)
        if has_init:
            hst[...] = h0_ref[d]
        else:
            hst[...] = jnp.zeros_like(hst)

        def chunk(ci, carry, d=d, tri=tri, tri_f=tri_f, tri_tf=tri_tf):
            c = ci if d == 0 else nc - 1 - ci
            rows = _chunk_rows(c, L)
            dt_c = _softplus(dtc_ref[c] + dtbc_ref[...])
            dt_r = _softplus(dtr_ref[c] + dtbr_ref[...])
            cs_c = jnp.dot(tri_f, dt_c * a_col, preferred_element_type=f32, precision=lax.Precision.HIGHEST)
            cs_r = jnp.dot(dt_r * a_row, tri_tf, preferred_element_type=f32, precision=lax.Precision.HIGHEST)
            cs_e = cs_c[L - 1:L, :] if d == 0 else cs_c[0:1, :]
            bm, cm = bc_sc[rows, :], cc_sc[rows, :]
            cb = lax.dot_general(cm, bm, (((1,), (1,)), ((), ())), preferred_element_type=f32)
            for p in range(n_pair):
                cols = slice(p * W, (p + 1) * W)
                ia, ib = d * E + 2 * p, d * E + 2 * p + 1
                xdt = xc_sc[rows, cols] * jnp.where(lo, dt_c[:, ia:ia + 1], dt_c[:, ib:ib + 1])
                xdt_bf = xdt.astype(bf16)
                cs_ta, cs_tb = cs_c[:, ia:ia + 1], cs_c[:, ib:ib + 1]
                wa = (cb * jnp.exp(jnp.where(tri, cs_ta - cs_r[ia:ia + 1, :], -jnp.inf))).astype(bf16)
                wb = (cb * jnp.exp(jnp.where(tri, cs_tb - cs_r[ib:ib + 1, :], -jnp.inf))).astype(bf16)
                y = jnp.where(lo, jnp.dot(wa, xdt_bf, preferred_element_type=f32),
                              jnp.dot(wb, xdt_bf, preferred_element_type=f32))
                hrows = slice(p * W, (p + 1) * W)
                hp = hst[hrows, :]
                if has_init or nc > 1:
                    y_inter = lax.dot_general(cm, hp.astype(bf16), (((1,), (1,)), ((), ())),
                                              preferred_element_type=f32)
                    y = y + jnp.where(lo, jnp.exp(cs_ta), jnp.exp(cs_tb)) * y_inter
                if d == 0:
                    yacc[rows, cols] = y
                else:
                    yacc[rows, cols] = yacc[rows, cols] + y
                if emit_state or nc > 1:
                    ea, eb = cs_e[:, ia:ia + 1], cs_e[:, ib:ib + 1]
                    xw = xdt * jnp.where(lo, jnp.exp(ea - cs_ta), jnp.exp(eb - cs_tb))
                    contrib = jnp.dot(xw.T.astype(bf16), bm, preferred_element_type=f32)
                    hst[hrows, :] = jnp.where(lo_rows, jnp.exp(ea), jnp.exp(eb)) * hp + contrib
            return carry

        if nc == 1:
            chunk(0, 0)
        else:
            lax.fori_loop(0, nc, chunk, 0)
        if emit_state:
            hout_ref[d] = hst[...]

    y = yacc[...] + dsk_ref[...] * xc_sc[...]
    z = z_ref[...].astype(f32)
    y = y * (z * jax.nn.sigmoid(z))
    y = y * lax.rsqrt(jnp.mean(y * y, axis=-1, keepdims=True) + EPS) * nw_ref[...]
    y_ref[...] = y.astype(y_ref.dtype)


def _ssd_seq(zx, dt_raw, conv_w, conv_b, dt_bias, a_log, d_skip, norm_w, layer, *, row0, nb, seq,
             init=None, emit_state=False):
    f32 = jnp.float32
    G, E, L, N = SSD_GROUPS, SSD_GROUP_HEADS, SSD_CHUNK, SSD_STATE
    gc = SSD_GROUP_COLS
    nc = seq // L
    rb0 = row0 // seq
    dt = dt_raw[row0:row0 + nb * seq].reshape(nb, nc, L, 2, G, E)
    dt_col = dt.transpose(0, 4, 1, 2, 3, 5).reshape(nb, G, nc, L, 2 * E)
    dt_row = dt.transpose(0, 4, 1, 3, 5, 2).reshape(nb, G, nc, 2 * E, L)

    def per_group(a):
        a = a.astype(f32).reshape(2, G, E).transpose(1, 0, 2).reshape(G, 2 * E)
        return a[:, None, :], a[:, :, None]

    dtb_c, dtb_r = per_group(dt_bias)
    al_c, al_r = per_group(a_log)
    x_cb = SSD_INNER // gc
    bc0 = 2 * SSD_INNER // N
    cw = conv_w.astype(f32)
    cbias = conv_b.astype(f32).reshape(1, -1)
    dsk = jnp.repeat(d_skip.astype(f32), SSD_HEAD_DIM).reshape(1, SSD_INNER)
    in_specs = [
        pl.BlockSpec((seq, gc), lambda b, g: (rb0 + b, x_cb + g)),
        pl.BlockSpec((seq, N), lambda b, g: (rb0 + b, bc0 + g)),
        pl.BlockSpec((seq, N), lambda b, g: (rb0 + b, bc0 + G + g)),
        pl.BlockSpec((seq, gc), lambda b, g: (rb0 + b, g)),
        pl.BlockSpec((None, None, nc, L, 2 * E), lambda b, g: (b, g, 0, 0, 0)),
        pl.BlockSpec((None, None, nc, 2 * E, L), lambda b, g: (b, g, 0, 0, 0)),
        pl.BlockSpec((None, 1, 2 * E), lambda b, g: (g, 0, 0)),
        pl.BlockSpec((None, 2 * E, 1), lambda b, g: (g, 0, 0)),
        pl.BlockSpec((None, 1, 2 * E), lambda b, g: (g, 0, 0)),
        pl.BlockSpec((None, 2 * E, 1), lambda b, g: (g, 0, 0)),
        pl.BlockSpec((SSD_CONV, gc), lambda b, g: (0, g)),
        pl.BlockSpec((1, gc), lambda b, g: (0, g)),
        pl.BlockSpec((SSD_CONV, N), lambda b, g: (0, SSD_INNER // N + g)),
        pl.BlockSpec((1, N), lambda b, g: (0, SSD_INNER // N + g)),
        pl.BlockSpec((SSD_CONV, N), lambda b, g: (0, SSD_INNER // N + G + g)),
        pl.BlockSpec((1, N), lambda b, g: (0, SSD_INNER // N + G + g)),
        pl.BlockSpec((1, gc), lambda b, g: (0, g)),
        pl.BlockSpec((1, gc), lambda b, g: (0, g)),
    ]
    args = [zx, zx, zx, zx, dt_col, dt_row, dtb_c, dtb_r, al_c, al_r,
            cw, cbias, cw, cbias, cw, cbias, dsk, norm_w.astype(f32).reshape(1, SSD_INNER)]
    has_init = init is not None
    state_block = pl.BlockSpec((None, None, 2, None, E * SSD_HEAD_DIM, N), lambda b, g: (b, layer, 0, g, 0, 0))
    if has_init:
        in_specs.append(state_block)
        args.append(init.reshape(*init.shape[:3], G, E * SSD_HEAD_DIM, N))
    out_specs = [pl.BlockSpec((seq, gc), lambda b, g: (b, g))]
    out_shape = [jax.ShapeDtypeStruct((nb * seq, SSD_INNER), jnp.bfloat16)]
    if emit_state:
        out_specs.append(state_block)
        out_shape.append(jax.ShapeDtypeStruct((nb, 1, 2, G, E * SSD_HEAD_DIM, N), f32))
    return pl.pallas_call(
        functools.partial(_ssd_kernel, seq=seq, has_init=has_init, emit_state=emit_state),
        name=f"ssd_s{seq}",
        grid=(nb, G),
        in_specs=in_specs,
        out_specs=out_specs,
        out_shape=out_shape,
        scratch_shapes=[
            pltpu.VMEM((seq, gc), f32),
            pltpu.VMEM((seq, N), jnp.bfloat16),
            pltpu.VMEM((seq, N), jnp.bfloat16),
            pltpu.VMEM((seq, gc), f32),
            pltpu.VMEM((E * SSD_HEAD_DIM, N), f32),
        ],
        compiler_params=_params("arbitrary", "arbitrary"),
    )(*args)


def _attn_kernel(*refs, heads, rope):
    f32, bf16 = jnp.float32, jnp.bfloat16
    if rope:
        qn_ref, qp_ref, kv_ref, kpe_ref, cos_ref, sin_ref, o_ref = refs
    else:
        qn_ref, qp_ref, kv_ref, kpe_ref, o_ref = refs
    scale = (MLA_NOPE + MLA_ROPE) ** -0.5
    kpe2 = kpe_ref[...]
    lq = qn_ref.shape[0]
    lane = lax.broadcasted_iota(jnp.int32, (lq, V7X_LANES), 1)
    nt = (((1,), (1,)), ((), ()))
    for pr in range(heads // 2):
        qp = qp_ref[:, pr * V7X_LANES:(pr + 1) * V7X_LANES]
        if rope:
            x = qp.astype(f32)
            other = jnp.where(lane % (2 * ROPE_FREQS) < ROPE_FREQS,
                              pltpu.roll(x, V7X_LANES - ROPE_FREQS, 1), pltpu.roll(x, ROPE_FREQS, 1))
            qp = (x * cos_ref[...] + other * sin_ref[...]).astype(bf16)
        for a in range(2):
            h = 2 * pr + a
            qpa = jnp.where((lane < MLA_ROPE) if a == 0 else (lane >= MLA_ROPE), qp, jnp.zeros_like(qp))
            kn = kv_ref[:, h * 2 * MLA_NOPE:h * 2 * MLA_NOPE + MLA_NOPE]
            vv = kv_ref[:, h * 2 * MLA_NOPE + MLA_NOPE:(h + 1) * 2 * MLA_NOPE]
            s = (lax.dot_general(qn_ref[:, h * MLA_NOPE:(h + 1) * MLA_NOPE], kn, nt, preferred_element_type=f32)
                 + lax.dot_general(qpa, kpe2, nt, preferred_element_type=f32)) * scale
            e = jnp.exp(s - jnp.max(s, axis=-1, keepdims=True))
            o = jnp.dot(e.astype(bf16), vv, preferred_element_type=f32) / jnp.sum(e, axis=-1, keepdims=True)
            o_ref[:, h * MLA_V:(h + 1) * MLA_V] = o.astype(o_ref.dtype)


def _attention(qn, qp, kv, kpe2, *, row0, nb, seq, key0, n_keys, q_tile, heads, tables=None):
    nq = seq // q_tile
    qb0 = row0 // q_tile
    kb0 = key0 // n_keys
    assert row0 % q_tile == 0 and key0 % n_keys == 0
    hb = MLA_HEADS // heads
    in_specs = [
        pl.BlockSpec((q_tile, heads * MLA_NOPE), lambda b, t, h: (qb0 + b * nq + t, h)),
        pl.BlockSpec((q_tile, heads * MLA_ROPE), lambda b, t, h: (qb0 + b * nq + t, h)),
        pl.BlockSpec((n_keys, heads * 2 * MLA_NOPE), lambda b, t, h: (kb0 + b, h)),
        pl.BlockSpec((n_keys, V7X_LANES), lambda b, t, h: (kb0 + b, 0)),
    ]
    args = [qn, qp, kv, kpe2]
    rope = tables is not None
    if rope:
        in_specs += [pl.BlockSpec((q_tile, V7X_LANES), lambda b, t, h: (t, 0))] * 2
        args += list(tables)
    return pl.pallas_call(
        functools.partial(_attn_kernel, heads=heads, rope=rope),
        name=f"attn_s{seq}",
        grid=(nb, nq, hb),
        in_specs=in_specs,
        out_specs=pl.BlockSpec((q_tile, heads * MLA_V), lambda b, t, h: (b * nq + t, h)),
        out_shape=jax.ShapeDtypeStruct((nb * seq, MLA_HEADS * MLA_V), jnp.bfloat16),
        compiler_params=_params("arbitrary", "arbitrary", "arbitrary"),
    )(*args)


def _rope_tables(n_tok):
    rows = n_tok // GRID_W
    row = jnp.repeat(jnp.arange(rows, dtype=jnp.float32), GRID_W)
    col = jnp.tile(jnp.arange(GRID_W, dtype=jnp.float32), rows)
    inv_freq = ROPE_THETA ** (-jnp.arange(ROPE_FREQS, dtype=jnp.float32) / ROPE_FREQS)
    ang = jnp.stack([row, col], axis=-1)[:, :, None] * inv_freq
    cos, sin = jnp.cos(ang), jnp.sin(ang)
    cos64 = jnp.concatenate([cos[:, 0], cos[:, 0], cos[:, 1], cos[:, 1]], axis=-1)
    sin64 = jnp.concatenate([-sin[:, 0], sin[:, 0], -sin[:, 1], sin[:, 1]], axis=-1)
    return jnp.tile(cos64, (1, 2)), jnp.tile(sin64, (1, 2))


def _rope_rows(x, cos64, sin64):
    xr = x.reshape(x.shape[0], 2, 2, ROPE_FREQS)
    other = jnp.stack([xr[:, :, 1], xr[:, :, 0]], axis=2).reshape(x.shape)
    return x * cos64 + other * sin64


def kernel(x_prompt, x_sample, state_mlstm_C, state_mlstm_n, state_mlstm_m, state_ssm, cache_mla_ckv, cache_mla_kpe, c, c_ctx, ada_w, ada_b, norm_mix, norm_ffn, mlstm_w_in, mlstm_b_gates, mlstm_norm, mlstm_w_out, ssd_w_in, ssd_conv_w, ssd_conv_b, ssd_dt_bias, ssd_A_log, ssd_D, ssd_norm, ssd_w_out, mla_w_in, mla_q_norm, mla_kv_norm, mla_w_qb, mla_w_kvb, mla_w_out, router_group_w, router_expert_w, moe_w_gate, moe_w_up, moe_w_down, final_norm):
    f32, bf16 = jnp.float32, jnp.bfloat16
    D = D_MODEL
    X = jnp.concatenate([x_prompt.reshape(N_PROMPT, D), x_sample.reshape(N_SAMPLE, D)], axis=0)

    cond = jnp.concatenate([c_ctx[None, :], c], axis=0)
    cond = jnp.pad(jax.nn.silu(cond), ((0, 16 - cond.shape[0]), (0, 0))).astype(bf16)

    n_mlstm = mlstm_w_in.shape[0]
    mlstm_state = None
    new_ssm, new_ckv, new_kpe = [], [], []
    for l in range(DEPTH):
        kind, j = l % N_MIXERS, l // N_MIXERS
        ada = _matmul(cond, ada_w, layer=l, tm=16, tn=512, out_dtype=f32)
        mods = (ada[:1 + DEC_BATCH] + ada_b[l]).reshape(1 + DEC_BATCH, N_MOD, 1, D)
        h = _norm(X, norm_mix[l], mods=mods, shift_chunk=0, scale_chunk=1)

        if kind == 0:
            proj = _matmul(h, mlstm_w_in, layer=j, n=12288)
            n_gate = 4 * MLSTM_HEADS
            gates = _matmul(h, mlstm_w_in[j, :, 12288:], tn=n_gate, out_dtype=f32) + mlstm_b_gates[j]
            outs = _mlstm_seq(proj, gates, mlstm_norm, j, row0=0, nb=BATCH, seq=SEQ, emit_state=True,
                              state_prev=mlstm_state)
            y_p, mlstm_state = outs[0], tuple(outs[1:])
            y_s = _mlstm_seq(proj, gates, mlstm_norm, j, row0=N_PROMPT, nb=DEC_BATCH, seq=DEC_SEQ,
                             init=(state_mlstm_C, state_mlstm_n, state_mlstm_m))[0]
            X = _matmul((y_p, y_s), mlstm_w_out, layer=j, tm=512, out_dtype=f32, res=X, mods=mods, gate_chunk=2)
        elif kind == 1:
            zx = _matmul(h, ssd_w_in, layer=j, n=18432)
            dt_raw = _matmul(h, ssd_w_in, layer=j, col0=18432, n=256, tn=256, out_dtype=f32)
            w = (ssd_conv_w[j], ssd_conv_b[j], ssd_dt_bias[j], ssd_A_log[j], ssd_D[j], ssd_norm[j])
            y_p, s_h = _ssd_seq(zx, dt_raw, *w, 0, row0=0, nb=BATCH, seq=SEQ, emit_state=True)
            y_s = _ssd_seq(zx, dt_raw, *w, j, row0=N_PROMPT, nb=DEC_BATCH, seq=DEC_SEQ, init=state_ssm)[0]
            new_ssm.append(s_h.reshape(BATCH, 1, 2, SSD_HEADS, SSD_HEAD_DIM, SSD_STATE))
            X = _matmul(jnp.concatenate([y_p, y_s], axis=0), ssd_w_out, layer=j, tm=512, tn=512, out_dtype=f32, res=X, mods=mods, gate_chunk=2,
                        single_buffer_w=True)
        else:
            lat = _matmul(h, mla_w_in, layer=j, n=2048, out_dtype=f32)
            kpe = _matmul(h, mla_w_in[j, :, 2048:], tn=MLA_ROPE, out_dtype=f32)
            hq = _norm(lat[:, :MLA_Q_RANK], mla_q_norm[j])
            ckv = _norm(lat[:, MLA_Q_RANK:], mla_kv_norm[j], out_dtype=f32)
            wq = mla_w_qb[j].reshape(MLA_Q_RANK, MLA_HEADS, MLA_NOPE + MLA_ROPE)
            qn = _matmul(hq, wq[:, :, :MLA_NOPE].reshape(MLA_Q_RANK, -1), tn=512)
            qp = _matmul(hq, wq[:, :, MLA_NOPE:].reshape(MLA_Q_RANK, -1), tn=512)
            n_keys = PAST_LEN + DEC_SEQ
            n_skeys = DEC_BATCH * n_keys
            cos2, sin2 = _rope_tables(DEC_SEQ)
            ckv_s = jnp.concatenate([cache_mla_ckv[:, j], ckv[N_PROMPT:].reshape(DEC_BATCH, DEC_SEQ, -1)], axis=1)
            ckv_all = jnp.concatenate([ckv_s.reshape(-1, MLA_KV_RANK), ckv[:N_PROMPT]], axis=0).astype(bf16)
            kv = _matmul(ckv_all, mla_w_kvb, layer=j, tn=512)
            kpe_lat = _rope_rows(kpe[N_PROMPT:], jnp.tile(cos2[:, :MLA_ROPE], (DEC_BATCH, 1)),
                                 jnp.tile(sin2[:, :MLA_ROPE], (DEC_BATCH, 1)))
            kpe_s = jnp.concatenate([cache_mla_kpe[:, j], kpe_lat.reshape(DEC_BATCH, DEC_SEQ, -1)], axis=1)
            kpe_all = jnp.concatenate([kpe_s.reshape(-1, MLA_ROPE), kpe[:N_PROMPT]], axis=0).astype(bf16)
            kpe2 = jnp.concatenate([kpe_all, kpe_all], axis=1)
            o_p = _attention(qn, qp, kv, kpe2, row0=0, nb=BATCH, seq=SEQ, key0=n_skeys, n_keys=SEQ, q_tile=SEQ,
                             heads=16)
            o_s = _attention(qn, qp, kv, kpe2, row0=N_PROMPT, nb=DEC_BATCH, seq=DEC_SEQ, key0=0,
                             n_keys=n_keys, q_tile=ATTN_Q_TILE, heads=2, tables=(cos2, sin2))
            new_ckv.append(ckv[:N_PROMPT].reshape(BATCH, 1, SEQ, MLA_KV_RANK))
            new_kpe.append(kpe[:N_PROMPT].reshape(BATCH, 1, SEQ, MLA_ROPE))
            X = _matmul(jnp.concatenate([o_p, o_s], axis=0), mla_w_out, layer=j, tm=512, tn=512, out_dtype=f32, res=X, mods=mods, gate_chunk=2,
                        single_buffer_w=True)

        router_w = jnp.pad(jnp.concatenate([router_group_w[l], router_expert_w[l]], axis=1),
                           ((0, 0), (0, ROUTER_LANES - MOE_GROUPS - N_EXPERTS)))
        hf, logits = _norm(X, norm_ffn[l], mods=mods, shift_chunk=3, scale_chunk=4, router_w=router_w,
                           out_dtype=jnp.uint32)
        X = _hier_moe(X, hf, logits, mods, 5, moe_w_gate, moe_w_up, moe_w_down, l)

    Y = _norm(X, final_norm, out_dtype=f32)
    y_prompt = Y[:N_PROMPT].reshape(BATCH, SEQ, D)
    y_sample = Y[N_PROMPT:].reshape(DEC_BATCH, DEC_SEQ, D)
    new_c, new_n, new_m = mlstm_state
    return (y_prompt, y_sample, new_c,
            new_n.reshape(BATCH, n_mlstm, 2, MLSTM_HEADS, MLSTM_DQK),
            new_m.reshape(BATCH, n_mlstm, 2, MLSTM_HEADS),
            jnp.concatenate(new_ssm, axis=1), jnp.concatenate(new_ckv, axis=1), jnp.concatenate(new_kpe, axis=1))
```
